```python
import math
import jax, jax.numpy as jnp
from jax import lax
import numpy as np

D_MODEL = 4096
BATCH = 4
SEQ = 2048
DEPTH = 4
DEC_BATCH = 8
DEC_SEQ = 1
PAST_LEN = 8192
PAGE_SIZE = 128

N_PAGES = PAST_LEN // PAGE_SIZE
N_POOL = (DEC_BATCH * N_PAGES * 5) // 4

N_EVEN = (DEPTH + 1) // 2
N_ODD = DEPTH // 2

D_FF = 11008
RMS_EPS = 1e-6

GLA_HEADS = 4
GLA_DV = (D_MODEL // 2) // GLA_HEADS
GLA_DK = GLA_DV // 2
GLA_RANK = 16
GLA_TAU = 16.0
GLA_CHUNK = 64
GLA_QK = GLA_HEADS * GLA_DK
GLA_V = GLA_HEADS * GLA_DV

CONV_CH = D_MODEL // 2
CONV_W = 3

EVEN_SPLITS = (GLA_QK, GLA_QK, GLA_V, GLA_V, GLA_RANK, CONV_CH, CONV_CH, CONV_CH)
EVEN_IN = GLA_QK * 2 + GLA_V * 2 + GLA_RANK + CONV_CH * 3
EVEN_MIX = GLA_V + CONV_CH

ATTN_HEADS = 16
ATTN_HD = D_MODEL // (2 * ATTN_HEADS)
ATTN_VD = 2 * ATTN_HD
ATTN_QK = ATTN_HEADS * 2 * ATTN_HD
ATTN_V = ATTN_HEADS * ATTN_VD
ATTN_SCALE = ATTN_HD ** -0.5
Q_BLOCK = 128
NEG_INF = -1e30

NUM_BUCKETS = 32
MAX_DISTANCE = 128

kernel_name = "hybrid_gla_shortconv_diffattn_macaron_decode_step"


def rmsnorm(x, gain):
    xf = x.astype(jnp.float32)
    y = xf * lax.rsqrt(jnp.mean(xf * xf, axis=-1, keepdims=True) + RMS_EPS)
    return (y * gain.astype(jnp.float32)).astype(x.dtype)


def half_ffn(x, gain, w_gu, w_down):
    h = rmsnorm(x, gain)
    gate, up = jnp.split(h @ w_gu, 2, axis=-1)
    return x + 0.5 * ((jax.nn.silu(gate) * up) @ w_down)


def gla_chunked(q, k, v, log_a, s0):
    b, t, h, _ = q.shape
    nc = t // GLA_CHUNK

    def to_chunks(a):
        return a.astype(jnp.float32).reshape(b, nc, GLA_CHUNK, h, a.shape[-1]).transpose(1, 0, 3, 2, 4)

    causal = jnp.tril(jnp.ones((GLA_CHUNK, GLA_CHUNK), dtype=bool))

    def step(s, inp):
        qc, kc, vc, gc = inp
        cum = jnp.cumsum(gc, axis=2)
        inter = jnp.einsum('bhid,bhde->bhie', qc * jnp.exp(cum), s)
        rel = cum[:, :, :, None, :] - cum[:, :, None, :, :]
        decay = jnp.exp(jnp.where(causal[:, :, None], rel, -jnp.inf))
        att = jnp.einsum('bhid,bhjd,bhijd->bhij', qc, kc, decay)
        o = inter + jnp.einsum('bhij,bhje->bhie', att, vc)
        last = cum[:, :, -1:, :]
        s = jnp.exp(last[:, :, 0, :])[..., None] * s + jnp.einsum('bhjd,bhje->bhde', kc * jnp.exp(last - cum), vc)
        return s, o

    s, o = lax.scan(step, s0.astype(jnp.float32), (to_chunks(q), to_chunks(k), to_chunks(v), to_chunks(log_a)))
    o = o.transpose(1, 0, 3, 2, 4).reshape(b, t, h, v.shape[-1])
    return o, s


def gla_recurrent(q, k, v, log_a, s0):
    def tm(a):
        return jnp.swapaxes(a.astype(jnp.float32), 0, 1)

    def step(s, inp):
        qt, kt, vt, gt = inp
        s = jnp.exp(gt)[..., None] * s + kt[..., None] * vt[..., None, :]
        return s, jnp.einsum('bhd,bhde->bhe', qt, s)

    s, o = lax.scan(step, s0.astype(jnp.float32), (tm(q), tm(k), tm(v), tm(log_a)))
    return jnp.swapaxes(o, 0, 1), s


def gla_conv_mixer(h, w_in, w_alpha, b_alpha, out_gain, conv_w, w_out, s0, conv_buf, chunked):
    b, t, _ = h.shape
    offs = np.cumsum(EVEN_SPLITS)[:-1].tolist()
    q, k, v, r, a_lo, gate_b, gate_c, h_in = jnp.split(h @ w_in, offs, axis=-1)
    q = q.reshape(b, t, GLA_HEADS, GLA_DK) * GLA_DK ** -0.5
    k = k.reshape(b, t, GLA_HEADS, GLA_DK)
    v = v.reshape(b, t, GLA_HEADS, GLA_DV)
    log_a = jax.nn.log_sigmoid((a_lo @ w_alpha + b_alpha).astype(jnp.float32)) / GLA_TAU
    log_a = log_a.reshape(b, t, GLA_HEADS, GLA_DK)
    gla = gla_chunked if chunked else gla_recurrent
    o, s_new = gla(q, k, v, log_a, s0)
    o = rmsnorm(o.astype(h.dtype), out_gain).reshape(b, t, GLA_V) * jax.nn.silu(r)
    u = gate_c * h_in
    u_ext = jnp.concatenate([conv_buf.astype(u.dtype), u], axis=1)
    conv = sum(conv_w[j] * u_ext[:, j:j + t] for j in range(CONV_W))
    y_conv = gate_b * conv
    out = jnp.concatenate([o, y_conv], axis=-1) @ w_out
    return out, s_new, u_ext[:, -(CONV_W - 1):]


def rel_bucket(rel):
    n = jnp.maximum(rel, 0)
    max_exact = NUM_BUCKETS // 2
    nf = jnp.maximum(n, 1).astype(jnp.float32)
    large = max_exact + (jnp.log(nf / max_exact) / math.log(MAX_DISTANCE / max_exact)
                         * (NUM_BUCKETS - max_exact)).astype(jnp.int32)
    large = jnp.minimum(large, NUM_BUCKETS - 1)
    return jnp.where(n < max_exact, n, large)


def attn_project(h, w_qkv):
    b, t, _ = h.shape
    q, k, v = jnp.split(h @ w_qkv, [ATTN_QK, 2 * ATTN_QK], axis=-1)
    return (q.reshape(b, t, ATTN_HEADS, 2, ATTN_HD),
            k.reshape(b, t, ATTN_HEADS, 2, ATTN_HD),
            v.reshape(b, t, ATTN_HEADS, ATTN_VD))


def diff_attn(q, k, v, q_pos, k_pos, lam, bias_table):
    s = jnp.einsum('bqhcd,bkhcd->bhcqk', q, k, preferred_element_type=jnp.float32) * ATTN_SCALE
    rel = q_pos[:, None] - k_pos[None, :]
    bias = jnp.transpose(bias_table[rel_bucket(rel)], (2, 0, 1)).astype(jnp.float32)
    s = jnp.where((rel >= 0)[None, None, None], s + bias[None, :, None], NEG_INF)
    p = jax.nn.softmax(s, axis=-1)
    a = p[:, :, 0] - lam * p[:, :, 1]
    return jnp.einsum('bhqk,bkhe->bqhe', a.astype(v.dtype), v)


def diff_attn_prompt(q, k, v, lam, bias_table):
    b, t = q.shape[:2]
    nb = t // Q_BLOCK
    qb = q.reshape(b, nb, Q_BLOCK, ATTN_HEADS, 2, ATTN_HD).transpose(1, 0, 2, 3, 4, 5)
    pos = jnp.arange(t, dtype=jnp.int32)
    pb = pos.reshape(nb, Q_BLOCK)
    ob = lax.map(lambda args: diff_attn(args[0], k, v, args[1], pos, lam, bias_table), (qb, pb))
    return ob.transpose(1, 0, 2, 3, 4).reshape(b, t, ATTN_HEADS, ATTN_VD)


def attn_output(o, subln, lam_init, w_o):
    b, t = o.shape[:2]
    o = rmsnorm(o, subln) * (1.0 - lam_init)
    return o.reshape(b, t, ATTN_V) @ w_o


def setup_inputs(seed: int = 0) -> dict:
    key = jax.random.key(seed)
    ks = jax.random.split(key, 32)
    f32 = jnp.float32

    def nrm(k, shape, scale):
        return jax.random.normal(k, shape, f32) * scale

    def gain(k, shape):
        return 1.0 + 0.05 * jax.random.normal(k, shape, f32)

    page_table = jax.random.permutation(ks[6], N_POOL)[:DEC_BATCH * N_PAGES].reshape(DEC_BATCH, N_PAGES).astype(jnp.int32)
    return {
        'x_prompt': nrm(ks[0], (BATCH, SEQ, D_MODEL), 1.0),
        'x_sample': nrm(ks[1], (DEC_BATCH, DEC_SEQ, D_MODEL), 1.0),
        'cache_k': nrm(ks[2], (N_ODD, N_POOL, PAGE_SIZE, ATTN_HEADS, 2 * ATTN_HD), 1.0),
        'cache_v': nrm(ks[3], (N_ODD, N_POOL, PAGE_SIZE, ATTN_HEADS, ATTN_VD), 1.0),
        'state_gla': nrm(ks[4], (N_EVEN, DEC_BATCH, GLA_HEADS, GLA_DK, GLA_DV), 0.5),
        'state_conv': nrm(ks[5], (N_EVEN, DEC_BATCH, CONV_W - 1, CONV_CH), 1.0),
        'page_table': page_table,
        'rel_bias': nrm(ks[7], (NUM_BUCKETS, ATTN_HEADS), 0.5),
        'norm_ffn1': gain(ks[8], (DEPTH, D_MODEL)),
        'w_ffn1_gu': nrm(ks[9], (DEPTH, D_MODEL, 2 * D_FF), D_MODEL ** -0.5),
        'w_ffn1_down': nrm(ks[10], (DEPTH, D_FF, D_MODEL), D_FF ** -0.5),
        'norm_mix': gain(ks[11], (DEPTH, D_MODEL)),
        'norm_ffn2': gain(ks[12], (DEPTH, D_MODEL)),
        'w_ffn2_gu': nrm(ks[13], (DEPTH, D_MODEL, 2 * D_FF), D_MODEL ** -0.5),
        'w_ffn2_down': nrm(ks[14], (DEPTH, D_FF, D_MODEL), D_FF ** -0.5),
        'norm_final': gain(ks[15], (D_MODEL,)),
        'even_w_in': nrm(ks[16], (N_EVEN, D_MODEL, EVEN_IN), D_MODEL ** -0.5),
        'gla_w_alpha': nrm(ks[17], (N_EVEN, GLA_RANK, GLA_QK), GLA_RANK ** -0.5),
        'gla_b_alpha': nrm(ks[18], (N_EVEN, GLA_QK), 0.1),
        'gla_out_norm': gain(ks[19], (N_EVEN, GLA_DV)),
        'short_conv_w': nrm(ks[20], (N_EVEN, CONV_W, CONV_CH), CONV_W ** -0.5),
        'even_w_out': nrm(ks[21], (N_EVEN, EVEN_MIX, D_MODEL), EVEN_MIX ** -0.5),
        'attn_w_qkv': nrm(ks[22], (N_ODD, D_MODEL, 2 * ATTN_QK + ATTN_V), D_MODEL ** -0.5),
        'attn_lambda_q1': nrm(ks[23], (N_ODD, ATTN_HD), 0.1),
        'attn_lambda_k1': nrm(ks[24], (N_ODD, ATTN_HD), 0.1),
        'attn_lambda_q2': nrm(ks[25], (N_ODD, ATTN_HD), 0.1),
        'attn_lambda_k2': nrm(ks[26], (N_ODD, ATTN_HD), 0.1),
        'attn_subln': gain(ks[27], (N_ODD, ATTN_VD)),
        'attn_w_o': nrm(ks[28], (N_ODD, ATTN_V, D_MODEL), ATTN_V ** -0.5),
    }


def reference(x_prompt, x_sample, cache_k, cache_v, state_gla, state_conv, page_table, rel_bias,
              norm_ffn1, w_ffn1_gu, w_ffn1_down, norm_mix, norm_ffn2, w_ffn2_gu, w_ffn2_down, norm_final,
              even_w_in, gla_w_alpha, gla_b_alpha, gla_out_norm, short_conv_w, even_w_out,
              attn_w_qkv, attn_lambda_q1, attn_lambda_k1, attn_lambda_q2, attn_lambda_k2, attn_subln, attn_w_o):
    xp, xs = x_prompt, x_sample
    bp, tp = xp.shape[:2]
    bs, ts = xs.shape[:2]
    past_len = page_table.shape[1] * PAGE_SIZE
    k_rows_p, v_rows_p, gla_p, conv_p = [], [], [], []
    k_rows_s, v_rows_s, gla_s, conv_s = [], [], [], []

    for l in range(DEPTH):
        xp = half_ffn(xp, norm_ffn1[l], w_ffn1_gu[l], w_ffn1_down[l])
        xs = half_ffn(xs, norm_ffn1[l], w_ffn1_gu[l], w_ffn1_down[l])
        hp = rmsnorm(xp, norm_mix[l])
        hs = rmsnorm(xs, norm_mix[l])
        if l % 2 == 0:
            e = l // 2
            prm = (even_w_in[e], gla_w_alpha[e], gla_b_alpha[e], gla_out_norm[e], short_conv_w[e], even_w_out[e])
            s_zero = jnp.zeros((bp, GLA_HEADS, GLA_DK, GLA_DV), jnp.float32)
            buf_zero = jnp.zeros((bp, CONV_W - 1, CONV_CH), xp.dtype)
            mp, s_p, b_p = gla_conv_mixer(hp, *prm, s_zero, buf_zero, True)
            ms, s_s, b_s = gla_conv_mixer(hs, *prm, state_gla[e], state_conv[e], False)
            gla_p.append(s_p.astype(state_gla.dtype))
            conv_p.append(b_p.astype(state_conv.dtype))
            gla_s.append(s_s.astype(state_gla.dtype))
            conv_s.append(b_s.astype(state_conv.dtype))
        else:
            a = l // 2
            lam_init = 0.8 - 0.6 * math.exp(-0.3 * l)
            lam = (jnp.exp(jnp.sum(attn_lambda_q1[a].astype(jnp.float32) * attn_lambda_k1[a].astype(jnp.float32)))
                   - jnp.exp(jnp.sum(attn_lambda_q2[a].astype(jnp.float32) * attn_lambda_k2[a].astype(jnp.float32)))
                   + lam_init)
            q_p, k_p, v_p = attn_project(hp, attn_w_qkv[a])
            o_p = diff_attn_prompt(q_p, k_p, v_p, lam, rel_bias)
            mp = attn_output(o_p, attn_subln[a], lam_init, attn_w_o[a])
            q_s, k_s, v_s = attn_project(hs, attn_w_qkv[a])
            k_past = cache_k[a, page_table].reshape(bs, past_len, ATTN_HEADS, 2, ATTN_HD)
            v_past = cache_v[a, page_table].reshape(bs, past_len, ATTN_HEADS, ATTN_VD)
            k_all = jnp.concatenate([k_past.astype(k_s.dtype), k_s], axis=1)
            v_all = jnp.concatenate([v_past.astype(v_s.dtype), v_s], axis=1)
            q_pos = past_len + jnp.arange(ts, dtype=jnp.int32)
            k_pos = jnp.arange(past_len + ts, dtype=jnp.int32)
            o_s = diff_attn(q_s, k_all, v_all, q_pos, k_pos, lam, rel_bias)
            ms = attn_output(o_s, attn_subln[a], lam_init, attn_w_o[a])
            k_rows_p.append(k_p.reshape(bp, tp, ATTN_HEADS, 2 * ATTN_HD).astype(cache_k.dtype))
            v_rows_p.append(v_p.astype(cache_v.dtype))
            k_rows_s.append(k_s.reshape(bs, ts, ATTN_HEADS, 2 * ATTN_HD).astype(cache_k.dtype))
            v_rows_s.append(v_s.astype(cache_v.dtype))
        xp = xp + mp
        xs = xs + ms
        xp = half_ffn(xp, norm_ffn2[l], w_ffn2_gu[l], w_ffn2_down[l])
        xs = half_ffn(xs, norm_ffn2[l], w_ffn2_gu[l], w_ffn2_down[l])

    y_prompt = rmsnorm(xp, norm_final)
    y_sample = rmsnorm(xs, norm_final)
    return (y_prompt, y_sample,
            jnp.stack(k_rows_p), jnp.stack(v_rows_p), jnp.stack(gla_p), jnp.stack(conv_p),
            jnp.stack(k_rows_s), jnp.stack(v_rows_s), jnp.stack(gla_s), jnp.stack(conv_s))
```

```python
import functools
import math

import numpy as np
import jax
import jax.numpy as jnp
from jax import lax
from jax.experimental import pallas as pl
from jax.experimental.pallas import tpu as pltpu

F32 = jnp.float32
BF16 = jnp.bfloat16

D_MODEL = 4096
D_FF = 11008
RMS_EPS = 1e-6
PAGE_SIZE = 128

GLA_HEADS = 4
GLA_DK = 256
GLA_DV = 512
GLA_RANK = 16
GLA_TAU = 16.0
GLA_CHUNK = 64
GLA_SUB = 16
GLA_QK = GLA_HEADS * GLA_DK
GLA_V = GLA_HEADS * GLA_DV
CONV_CH = 2048
CONV_W = 3

ATTN_HEADS = 16
ATTN_HD = 128
ATTN_VD = 256
ATTN_QK = ATTN_HEADS * 2 * ATTN_HD
ATTN_SCALE = ATTN_HD ** -0.5
NEG_INF = -1e30
NUM_BUCKETS = 32
MAX_DISTANCE = 128

LANES = 128
FFN_TF = 256
VMEM_LIMIT = 56 * 1024 * 1024


def _bucket_steps():
    max_exact = NUM_BUCKETS // 2
    table = []
    n = 0
    while True:
        if n < max_exact:
            b = n
        else:
            b = min(max_exact + int(math.log(n / max_exact) / math.log(MAX_DISTANCE / max_exact)
                                    * (NUM_BUCKETS - max_exact)), NUM_BUCKETS - 1)
        table.append(b)
        if b == NUM_BUCKETS - 1:
            break
        n += 1
    steps = [(i, table[i]) for i in range(1, len(table)) if table[i] != table[i - 1]]
    return table[0], steps


_BUCKET0, _BUCKET_STEPS = _bucket_steps()


def _dot(a, b):
    return jnp.dot(a, b, preferred_element_type=F32)


def _dot_nt(a, b):
    return lax.dot_general(a, b, (((1,), (1,)), ((), ())), preferred_element_type=F32)


def _dot_tn(a, b):
    return lax.dot_general(a, b, (((0,), (0,)), ((), ())), preferred_element_type=F32)


def _params(*sem):
    return pltpu.CompilerParams(dimension_semantics=sem, vmem_limit_bytes=VMEM_LIMIT)


def _silu(x):
    return x * (1.0 / (1.0 + jnp.exp(-x)))


def _rms_rows(x, gain):
    ms = jnp.mean(x * x, axis=-1, keepdims=True)
    return x * lax.rsqrt(ms + RMS_EPS) * gain


def _rmsnorm_body(x_ref, g_ref, o_ref):
    o_ref[...] = _rms_rows(x_ref[...], g_ref[...]).astype(o_ref.dtype)


def rmsnorm(x, gain, out_dtype):
    m, d = x.shape
    tm = min(m, 256)
    return pl.pallas_call(
        _rmsnorm_body,
        grid=(m // tm,),
        in_specs=[pl.BlockSpec((tm, d), lambda i: (i, 0)),
                  pl.BlockSpec((1, d), lambda i: (0, 0))],
        out_specs=pl.BlockSpec((tm, d), lambda i: (i, 0)),
        out_shape=jax.ShapeDtypeStruct((m, d), out_dtype),
        compiler_params=_params("arbitrary"),
        name="rmsnorm",
    )(x, gain.reshape(1, d))


def _ffn_body(x_ref, g_ref, wg_ref, wu_ref, wd_ref, o_ref, xn_ref, *, tm, nj):
    j = pl.program_id(1)
    rc = min(tm, 32)

    @pl.when(j == 0)
    def _():
        def body(r, c):
            rows = pl.ds(pl.multiple_of(r * rc, rc), rc)
            xn_ref[rows, :] = _rms_rows(x_ref[rows, :], g_ref[...]).astype(BF16)
            o_ref[rows, :] = jnp.zeros((rc, o_ref.shape[1]), F32)
            return c
        lax.fori_loop(0, tm // rc, body, 0)

    xn = xn_ref[...]
    g = _dot(xn, wg_ref[...])
    u = _dot(xn, wu_ref[...])
    a = (_silu(g) * u).astype(BF16)
    o_ref[...] += _dot(a, wd_ref[...])

    @pl.when(j == nj - 1)
    def _():
        def body(r, c):
            rows = pl.ds(pl.multiple_of(r * rc, rc), rc)
            o_ref[rows, :] = x_ref[rows, :] + 0.5 * o_ref[rows, :]
            return c
        lax.fori_loop(0, tm // rc, body, 0)


def half_ffn(x, gain, w_gu, w_down, layer, tm):
    m, d = x.shape
    nj = D_FF // FFN_TF
    return pl.pallas_call(
        functools.partial(_ffn_body, tm=tm, nj=nj),
        grid=(m // tm, nj),
        in_specs=[pl.BlockSpec((tm, d), lambda i, j: (i, 0), pipeline_mode=pl.Buffered(1)),
                  pl.BlockSpec((1, d), lambda i, j: (0, 0)),
                  pl.BlockSpec((None, d, FFN_TF), lambda i, j: (layer, 0, j)),
                  pl.BlockSpec((None, d, FFN_TF), lambda i, j: (layer, 0, j + nj)),
                  pl.BlockSpec((None, FFN_TF, d), lambda i, j: (layer, j, 0))],
        out_specs=pl.BlockSpec((tm, d), lambda i, j: (i, 0), pipeline_mode=pl.Buffered(1)),
        out_shape=jax.ShapeDtypeStruct((m, d), F32),
        scratch_shapes=[pltpu.VMEM((tm, d), BF16)],
        compiler_params=_params("arbitrary", "arbitrary"),
        name="half_ffn",
    )(x, gain.reshape(1, d), w_gu, w_gu, w_down)


def _mm_body(*refs, n_a, has_res, has_scale, n_out):
    a_refs = refs[:n_a]
    w_ref = refs[n_a]
    pos = n_a + 1
    scale_ref = res_ref = None
    if has_scale:
        scale_ref = refs[pos]
        pos += 1
    if has_res:
        res_ref = refs[pos]
        pos += 1
    out_refs = refs[pos:pos + n_out]
    acc = None
    k0 = 0
    for a_ref in a_refs:
        kk = a_ref.shape[1]
        part = _dot(a_ref[...], w_ref[k0:k0 + kk, :])
        acc = part if acc is None else acc + part
        k0 += kk
    if has_res:
        acc = acc + res_ref[...]
    for o_ref in out_refs:
        if has_scale and o_ref.dtype == BF16:
            o_ref[...] = (acc * scale_ref[...]).astype(BF16)
        else:
            o_ref[...] = acc.astype(o_ref.dtype)


def matmul(a_list, w, layer, col0, n, out_dtypes, tm, tn, res=None, scale=None):
    m = a_list[0].shape[0]
    k_total = w.shape[1]
    joff = col0 // tn
    in_specs = [pl.BlockSpec((tm, a.shape[1]), lambda i, j: (i, 0)) for a in a_list]
    in_specs.append(pl.BlockSpec((None, k_total, tn), lambda i, j: (layer, 0, j + joff)))
    args = list(a_list) + [w]
    if scale is not None:
        in_specs.append(pl.BlockSpec((1, tn), lambda i, j: (0, j)))
        args.append(scale)
    if res is not None:
        in_specs.append(pl.BlockSpec((tm, tn), lambda i, j: (i, j)))
        args.append(res)
    outs = pl.pallas_call(
        functools.partial(_mm_body, n_a=len(a_list), has_res=res is not None,
                          has_scale=scale is not None, n_out=len(out_dtypes)),
        grid=(m // tm, n // tn),
        in_specs=in_specs,
        out_specs=[pl.BlockSpec((tm, tn), lambda i, j: (i, j)) for _ in out_dtypes],
        out_shape=[jax.ShapeDtypeStruct((m, n), dt) for dt in out_dtypes],
        compiler_params=_params("arbitrary", "arbitrary"),
        name="matmul",
    )(*args)
    return outs


def _log_decay(alo, walpha, balpha):
    z = _dot(alo.astype(BF16), walpha) + balpha
    return -(jnp.maximum(-z, 0.0) + jnp.log1p(jnp.exp(-jnp.abs(z)))) / GLA_TAU


def _cumsum_rows(x):
    n = x.shape[0]
    rows = lax.broadcasted_iota(jnp.int32, x.shape, 0)
    s = 1
    while s < n:
        x = x + jnp.where(rows >= s, pltpu.roll(x, s, 0), 0.0)
        s *= 2
    return x


def _gla_chunk_body(q_ref, k_ref, v_ref, r_ref, alo_ref, wa_ref, ba_ref, gain_ref, o_ref, s_ref, st_ref):
    c = pl.program_id(2)
    nc = pl.num_programs(2)
    C, SUB = GLA_CHUNK, GLA_SUB

    @pl.when(c == 0)
    def _():
        st_ref[...] = jnp.zeros_like(st_ref)

    q = q_ref[...] * GLA_DK ** -0.5
    k = k_ref[...]
    v = v_ref[...]
    vb = v.astype(BF16)
    cum = _cumsum_rows(_log_decay(alo_ref[...], wa_ref[...], ba_ref[...]))
    last = cum[C - 1:C, :]
    st = st_ref[...]

    inter = _dot_nt((q * jnp.exp(cum)).astype(BF16), st.astype(BF16))

    rows_c = lax.broadcasted_iota(jnp.int32, (C, GLA_DK), 0)
    lane_c = lax.broadcasted_iota(jnp.int32, (SUB, C), 1)
    sub_j = lax.broadcasted_iota(jnp.int32, (SUB, GLA_DK), 0)
    diag_blocks = []
    att_t = jnp.zeros((C, C), F32)
    for blk in range(C // SUB):
        lo = blk * SUB
        q_b, k_b, c_b = q[lo:lo + SUB], k[lo:lo + SUB], cum[lo:lo + SUB]
        dg = jnp.zeros((SUB, C), F32)
        for i in range(SUB):
            rel = c_b[i:i + 1, :] - c_b
            dec = jnp.exp(jnp.where(sub_j <= i, rel, -jnp.inf))
            col = jnp.sum(q_b[i:i + 1, :] * k_b * dec, axis=-1, keepdims=True)
            dg = jnp.where(lane_c == lo + i, col, dg)
        diag_blocks.append(dg)
        if blk > 0:
            edge = cum[lo - 1:lo, :]
            in_blk = (rows_c >= lo) & (rows_c < lo + SUB)
            q_t = jnp.where(in_blk, q * jnp.exp(jnp.where(in_blk, cum - edge, 0.0)), 0.0)
            k_t = k[:lo] * jnp.exp(edge - cum[:lo])
            off = _dot_nt(k_t.astype(BF16), q_t.astype(BF16))
            att_t = att_t + jnp.concatenate([off, jnp.zeros((C - lo, C), F32)], axis=0)
    att_t = att_t + jnp.concatenate(diag_blocks, axis=0)

    o = inter + _dot_tn(att_t.astype(BF16), vb)
    y = _rms_rows(o, gain_ref[...]) * _silu(r_ref[...])
    o_ref[...] = y.astype(o_ref.dtype)

    k_l = (k * jnp.exp(last - cum)).astype(BF16)
    st_new = st * jnp.exp(last) + _dot_tn(vb, k_l)
    st_ref[...] = st_new

    @pl.when(c == nc - 1)
    def _():
        s_ref[...] = st_new.T


def gla_prompt(proj, alo, w_alpha, b_alpha, out_gain, e, batch, seq):
    m = proj.shape[0]
    nc = seq // GLA_CHUNK
    C = GLA_CHUNK
    row = lambda b, h, c: b * nc + c
    return pl.pallas_call(
        _gla_chunk_body,
        grid=(batch, GLA_HEADS, nc),
        in_specs=[pl.BlockSpec((C, GLA_DK), lambda b, h, c: (row(b, h, c), h)),
                  pl.BlockSpec((C, GLA_DK), lambda b, h, c: (row(b, h, c), GLA_HEADS + h)),
                  pl.BlockSpec((C, GLA_DV), lambda b, h, c: (row(b, h, c), GLA_HEADS + h)),
                  pl.BlockSpec((C, GLA_DV), lambda b, h, c: (row(b, h, c), 2 * GLA_HEADS + h)),
                  pl.BlockSpec((C, LANES), lambda b, h, c: (row(b, h, c), 0)),
                  pl.BlockSpec((None, LANES, GLA_DK), lambda b, h, c: (e, 0, h)),
                  pl.BlockSpec((None, 1, GLA_DK), lambda b, h, c: (e, 0, h)),
                  pl.BlockSpec((None, 1, GLA_DV), lambda b, h, c: (e, 0, 0))],
        out_specs=[pl.BlockSpec((C, GLA_DV), lambda b, h, c: (row(b, h, c), h)),
                   pl.BlockSpec((None, None, GLA_DK, GLA_DV), lambda b, h, c: (b, h, 0, 0))],
        out_shape=[jax.ShapeDtypeStruct((m, GLA_V), BF16),
                   jax.ShapeDtypeStruct((batch, GLA_HEADS, GLA_DK, GLA_DV), F32)],
        scratch_shapes=[pltpu.VMEM((GLA_DV, GLA_DK), F32)],
        compiler_params=_params("arbitrary", "arbitrary", "arbitrary"),
        name="gla_prompt",
    )(proj, proj, proj, proj, alo, w_alpha, b_alpha, out_gain)


def _row_to_col(row, n):
    r = lax.broadcasted_iota(jnp.int32, (n, n), 0)
    c = lax.broadcasted_iota(jnp.int32, (n, n), 1)
    return jnp.sum(jnp.where(r == c, row, 0.0), axis=1, keepdims=True)


def _gla_step_body(q_ref, k_ref, v_ref, r_ref, alo_ref, wa_ref, ba_ref, gain_ref, s0_ref, o_ref, s_ref):
    b = pl.program_id(0)

    def pick(ref):
        blk = ref[...]
        rows = lax.broadcasted_iota(jnp.int32, blk.shape, 0)
        return jnp.sum(jnp.where(rows == b, blk, 0.0), axis=0, keepdims=True)

    q = pick(q_ref) * GLA_DK ** -0.5
    k = pick(k_ref)
    v = pick(v_ref)
    r = pick(r_ref)
    g = _log_decay(pick(alo_ref), wa_ref[...], ba_ref[...])
    a_col = _row_to_col(jnp.exp(g), GLA_DK)
    k_col = _row_to_col(k, GLA_DK)
    q_col = _row_to_col(q, GLA_DK)
    s_new = a_col * s0_ref[...] + k_col * v
    s_ref[...] = s_new
    o = jnp.sum(q_col * s_new, axis=0, keepdims=True)
    o_ref[...] = _rms_rows(o, gain_ref[...]) * _silu(r)


def gla_sample(proj, alo, w_alpha, b_alpha, out_gain, state, e, batch):
    rows = proj.shape[0]
    return pl.pallas_call(
        _gla_step_body,
        grid=(batch, GLA_HEADS),
        in_specs=[pl.BlockSpec((rows, GLA_DK), lambda b, h: (0, h)),
                  pl.BlockSpec((rows, GLA_DK), lambda b, h: (0, GLA_HEADS + h)),
                  pl.BlockSpec((rows, GLA_DV), lambda b, h: (0, GLA_HEADS + h)),
                  pl.BlockSpec((rows, GLA_DV), lambda b, h: (0, 2 * GLA_HEADS + h)),
                  pl.BlockSpec((rows, LANES), lambda b, h: (0, 0)),
                  pl.BlockSpec((None, LANES, GLA_DK), lambda b, h: (e, 0, h)),
                  pl.BlockSpec((None, 1, GLA_DK), lambda b, h: (e, 0, h)),
                  pl.BlockSpec((None, 1, GLA_DV), lambda b, h: (e, 0, 0)),
                  pl.BlockSpec((None, None, None, GLA_DK, GLA_DV), lambda b, h: (e, b, h, 0, 0))],
        out_specs=[pl.BlockSpec((None, 1, GLA_DV), lambda b, h: (b, 0, h)),
                   pl.BlockSpec((None, None, GLA_DK, GLA_DV), lambda b, h: (b, h, 0, 0))],
        out_shape=[jax.ShapeDtypeStruct((batch, 1, GLA_V), F32),
                   jax.ShapeDtypeStruct((batch, GLA_HEADS, GLA_DK, GLA_DV), F32)],
        compiler_params=_params("arbitrary", "arbitrary"),
        name="gla_sample",
    )(proj, proj, proj, proj, alo, w_alpha, b_alpha, out_gain, state)


CONV_TT = 256
CONV_HALO = 8


def _conv_prompt_body(gb_ref, gc_ref, hin_ref, pc_ref, ph_ref, w_ref, y_ref, buf_ref, u_ref):
    i = pl.program_id(1)
    nt = pl.num_programs(1)
    T, H = CONV_TT, CONV_HALO
    u = gc_ref[...] * hin_ref[...]
    prev = pc_ref[...] * ph_ref[...]
    u_ref[0:H, :] = jnp.where(i > 0, prev, 0.0)
    u_ref[H:H + T, :] = u
    w = w_ref[...]
    conv = w[0:1, :] * u_ref[H - 2:H - 2 + T, :] + w[1:2, :] * u_ref[H - 1:H - 1 + T, :] + w[2:3, :] * u
    y_ref[...] = (gb_ref[...] * conv).astype(y_ref.dtype)

    @pl.when(i == nt - 1)
    def _():
        buf_ref[...] = u_ref[H + T - (CONV_W - 1):H + T, :]


def conv_prompt(proj, conv_w, e, batch, seq):
    m = proj.shape[0]
    T, H = CONV_TT, CONV_HALO
    nt = seq // T
    row = lambda b, i: b * nt + i
    prev = lambda b, i: jnp.maximum((b * seq + i * T) // H - 1, 0)
    return pl.pallas_call(
        _conv_prompt_body,
        grid=(batch, nt),
        in_specs=[pl.BlockSpec((T, CONV_CH), lambda b, i: (row(b, i), 3)),
                  pl.BlockSpec((T, CONV_CH), lambda b, i: (row(b, i), 4)),
                  pl.BlockSpec((T, CONV_CH), lambda b, i: (row(b, i), 5)),
                  pl.BlockSpec((H, CONV_CH), lambda b, i: (prev(b, i), 4)),
                  pl.BlockSpec((H, CONV_CH), lambda b, i: (prev(b, i), 5)),
                  pl.BlockSpec((None, CONV_W, CONV_CH), lambda b, i: (e, 0, 0))],
        out_specs=[pl.BlockSpec((T, CONV_CH), lambda b, i: (row(b, i), 0)),
                   pl.BlockSpec((None, CONV_W - 1, CONV_CH), lambda b, i: (b, 0, 0))],
        out_shape=[jax.ShapeDtypeStruct((m, CONV_CH), BF16),
                   jax.ShapeDtypeStruct((batch, CONV_W - 1, CONV_CH), F32)],
        scratch_shapes=[pltpu.VMEM((H + T, CONV_CH), F32)],
        compiler_params=_params("arbitrary", "arbitrary"),
        name="conv_prompt",
    )(proj, proj, proj, proj, proj, conv_w)


def _conv_sample_body(gb_ref, gc_ref, hin_ref, buf_ref, w_ref, y_ref, nbuf_ref):
    u = gc_ref[...] * hin_ref[...]
    w = w_ref[...]
    b0 = buf_ref[0]
    b1 = buf_ref[1]
    y_ref[...] = gb_ref[...] * (w[0:1, :] * b0 + w[1:2, :] * b1 + w[2:3, :] * u)
    nbuf_ref[0] = b1
    nbuf_ref[1] = u


def conv_sample(proj, buf_t, conv_w, e):
    rows = proj.shape[0]
    return pl.pallas_call(
        _conv_sample_body,
        grid=(1,),
        in_specs=[pl.BlockSpec((rows, CONV_CH), lambda i: (0, 3)),
                  pl.BlockSpec((rows, CONV_CH), lambda i: (0, 4)),
                  pl.BlockSpec((rows, CONV_CH), lambda i: (0, 5)),
                  pl.BlockSpec((CONV_W - 1, rows, CONV_CH), lambda i: (0, 0, 0)),
                  pl.BlockSpec((None, CONV_W, CONV_CH), lambda i: (e, 0, 0))],
        out_specs=[pl.BlockSpec((rows, CONV_CH), lambda i: (0, 0)),
                   pl.BlockSpec((CONV_W - 1, rows, CONV_CH), lambda i: (0, 0, 0))],
        out_shape=[jax.ShapeDtypeStruct((rows, CONV_CH), F32),
                   jax.ShapeDtypeStruct((CONV_W - 1, rows, CONV_CH), F32)],
        compiler_params=_params("arbitrary"),
        name="conv_sample",
    )(proj, proj, proj, buf_t, conv_w)


def _bias_of_distance(n, bias_at):
    t = bias_at(_BUCKET0)
    for start, bucket in _BUCKET_STEPS:
        t = jnp.where(n >= start, bias_at(bucket), t)
    return t


def _lambda_full(lq1_ref, lk1_ref, lq2_ref, lk2_ref, lam_init):
    s1 = jnp.sum(lq1_ref[...] * lk1_ref[...], axis=-1, keepdims=True)
    s2 = jnp.sum(lq2_ref[...] * lk2_ref[...], axis=-1, keepdims=True)
    return jnp.exp(s1) - jnp.exp(s2) + lam_init


def _softmax_step(s, m, l, acc, vb):
    m_new = jnp.maximum(m, jnp.max(s, axis=-1, keepdims=True))
    alpha = jnp.exp(m - m_new)
    p = jnp.exp(s - m_new)
    l_new = alpha * l + jnp.sum(p, axis=-1, keepdims=True)
    acc_new = alpha * acc + _dot(p.astype(BF16), vb)
    return m_new, l_new, acc_new


ATT_T = 256


def _attn_prompt_body(bias_ref, q_ref, k_ref, v_ref, lq1_ref, lk1_ref, lq2_ref, lk2_ref, sub_ref, o_ref,
                      tb_ref, *, lam_init):
    h = pl.program_id(0)
    b = pl.program_id(1)
    qi = pl.program_id(2)
    T = ATT_T

    @pl.when((b == 0) & (qi == 0))
    def _():
        n0 = lax.broadcasted_iota(jnp.int32, (T, T), 0) - lax.broadcasted_iota(jnp.int32, (T, T), 1)
        bias_at = lambda bucket: bias_ref[bucket, h]
        tb_ref[0] = jnp.where(n0 >= 0, _bias_of_distance(n0, bias_at), NEG_INF)
        tb_ref[1] = _bias_of_distance(n0 + T, bias_at)

    q = q_ref[...]
    qs = (q[:, :ATTN_HD], q[:, ATTN_HD:])
    far_bias = bias_ref[NUM_BUCKETS - 1, h]

    def tile(j, carry, bias):
        rows = pl.ds(pl.multiple_of(j * T, T), T)
        kb = k_ref[rows, :]
        vb = v_ref[rows, :]
        out = []
        for c in range(2):
            m, l, acc = carry[c]
            s = _dot_nt(qs[c], kb[:, c * ATTN_HD:(c + 1) * ATTN_HD]) + bias
            out.append(_softmax_step(s, m, l, acc, vb))
        return tuple(out)

    init = tuple((jnp.full((T, 1), -jnp.inf, F32), jnp.zeros((T, 1), F32), jnp.zeros((T, ATTN_VD), F32))
                 for _ in range(2))
    carry = lax.fori_loop(0, jnp.maximum(qi - 1, 0), lambda j, c: tile(j, c, far_bias), init)
    carry = lax.fori_loop(jnp.where(qi > 0, 0, 1), 2, lambda t, c: tile(qi - 1 + t, c, tb_ref[1 - t]), carry)

    lam = _lambda_full(lq1_ref, lk1_ref, lq2_ref, lk2_ref, lam_init)
    (m0, l0, a0), (m1, l1, a1) = carry
    o = a0 * (1.0 / l0) - lam * (a1 * (1.0 / l1))
    o_ref[...] = (_rms_rows(o, sub_ref[...]) * (1.0 - lam_init)).astype(o_ref.dtype)


def attn_prompt(qb, kb, vb, rel_bias, lams, subln, a, lam_init, batch, seq):
    m = qb.shape[0]
    T = ATT_T
    nq = seq // T
    lam_spec = pl.BlockSpec((None, 1, ATTN_HD), lambda h, b, i: (a, 0, 0))
    return pl.pallas_call(
        functools.partial(_attn_prompt_body, lam_init=lam_init),
        grid=(ATTN_HEADS, batch, nq),
        in_specs=[pl.BlockSpec(memory_space=pltpu.SMEM),
                  pl.BlockSpec((T, ATTN_VD), lambda h, b, i: (b * nq + i, h)),
                  pl.BlockSpec((seq, ATTN_VD), lambda h, b, i: (b, h)),
                  pl.BlockSpec((seq, ATTN_VD), lambda h, b, i: (b, h)),
                  lam_spec, lam_spec, lam_spec, lam_spec,
                  pl.BlockSpec((None, 1, ATTN_VD), lambda h, b, i: (a, 0, 0))],
        out_specs=pl.BlockSpec((T, ATTN_VD), lambda h, b, i: (b * nq + i, h)),
        out_shape=jax.ShapeDtypeStruct((m, ATTN_HEADS * ATTN_VD), BF16),
        scratch_shapes=[pltpu.VMEM((2, T, T), F32)],
        compiler_params=_params("arbitrary", "arbitrary", "arbitrary"),
        name="attn_prompt",
    )(rel_bias, qb, kb, vb, *lams, subln)


def _attn_sample_body(pt_ref, qm_ref, k_ref, v_ref, kn_ref, vn_ref, bias_ref, lq1_ref, lk1_ref, lq2_ref, lk2_ref,
                      sub_ref, o_ref, m_ref, l_ref, acc_ref, *, lam_init, n_pages):
    p = pl.program_id(1)
    H = ATTN_HEADS
    R = 2 * H
    P = PAGE_SIZE
    qm = qm_ref[...]

    @pl.when(p == 0)
    def _():
        m_ref[...] = jnp.full(m_ref.shape, -jnp.inf, F32)
        l_ref[...] = jnp.zeros_like(l_ref)
        acc_ref[...] = jnp.zeros_like(acc_ref)

    def update(k2d, v2d, bias):
        n = k2d.shape[0]
        row_h = lax.broadcasted_iota(jnp.int32, (R, n), 0) % H
        col_h = lax.broadcasted_iota(jnp.int32, (R, n), 1) % H
        s = jnp.where(row_h == col_h, _dot_nt(qm, k2d) + bias, NEG_INF)
        m_new, l_new, acc_new = _softmax_step(s, m_ref[...], l_ref[...], acc_ref[...], v2d)
        m_ref[...] = m_new
        l_ref[...] = l_new
        acc_ref[...] = acc_new

    def page(bias):
        k2d = k_ref[...].reshape(P * H, 2 * ATTN_HD).astype(BF16)
        v2d = v_ref[...].reshape(P * H, ATTN_VD).astype(BF16)
        update(k2d, v2d, bias)

    @pl.when(p < n_pages - 1)
    def _():
        page(bias_ref[:, NUM_BUCKETS - 1:NUM_BUCKETS])

    @pl.when(p == n_pages - 1)
    def _():
        tok = lax.broadcasted_iota(jnp.int32, (R, P * H), 1) // H
        dist = P - tok
        page(_bias_of_distance(dist, lambda bucket: bias_ref[:, bucket:bucket + 1]))

    @pl.when(p == n_pages)
    def _():
        update(kn_ref[...].astype(BF16), vn_ref[...].astype(BF16), bias_ref[:, 0:1])
        lam = _lambda_full(lq1_ref, lk1_ref, lq2_ref, lk2_ref, lam_init)
        on = acc_ref[...] * (1.0 / l_ref[...])
        o = on[:H] - lam * on[H:]
        o_ref[...] = _rms_rows(o, sub_ref[...]) * (1.0 - lam_init)


def attn_sample(qm, cache_k, cache_v, k_new, v_new, page_table, bias_rows, lams, subln, a, lam_init):
    batch, n_pages = page_table.shape
    H, P = ATTN_HEADS, PAGE_SIZE
    R = 2 * H
    pg = lambda b, p, pt: (a, pt[b, jnp.minimum(p, n_pages - 1)], 0, 0, 0)
    lam_spec = pl.BlockSpec((None, 1, ATTN_HD), lambda b, p, pt: (a, 0, 0))
    grid_spec = pltpu.PrefetchScalarGridSpec(
        num_scalar_prefetch=1,
        grid=(batch, n_pages + 1),
        in_specs=[pl.BlockSpec((None, R, 2 * ATTN_HD), lambda b, p, pt: (b, 0, 0)),
                  pl.BlockSpec((None, None, P, H, 2 * ATTN_HD), pg),
                  pl.BlockSpec((None, None, P, H, ATTN_VD), pg),
                  pl.BlockSpec((None, H, 2 * ATTN_HD), lambda b, p, pt: (b, 0, 0)),
                  pl.BlockSpec((None, H, ATTN_VD), lambda b, p, pt: (b, 0, 0)),
                  pl.BlockSpec((R, NUM_BUCKETS), lambda b, p, pt: (0, 0)),
                  lam_spec, lam_spec, lam_spec, lam_spec,
                  pl.BlockSpec((None, 1, ATTN_VD), lambda b, p, pt: (a, 0, 0))],
        out_specs=pl.BlockSpec((None, H, ATTN_VD), lambda b, p, pt: (b, 0, 0)),
        scratch_shapes=[pltpu.VMEM((R, 1), F32), pltpu.VMEM((R, 1), F32), pltpu.VMEM((R, ATTN_VD), F32)],
    )
    return pl.pallas_call(
        functools.partial(_attn_sample_body, lam_init=lam_init, n_pages=n_pages),
        grid_spec=grid_spec,
        out_shape=jax.ShapeDtypeStruct((batch, H, ATTN_VD), F32),
        compiler_params=_params("arbitrary", "arbitrary"),
        name="attn_sample",
    )(page_table, qm, cache_k, cache_v, k_new, v_new, bias_rows, *lams, subln)


PROMPT_TM = 1024
SAMPLE_ROWS = 16
MM_TN = 512


def _pad_rows(x, rows):
    return jnp.pad(x, ((0, rows - x.shape[0]), (0, 0)))


def kernel(x_prompt, x_sample, cache_k, cache_v, state_gla, state_conv, page_table, rel_bias, norm_ffn1, w_ffn1_gu, w_ffn1_down, norm_mix, norm_ffn2, w_ffn2_gu, w_ffn2_down, norm_final, even_w_in, gla_w_alpha, gla_b_alpha, gla_out_norm, short_conv_w, even_w_out, attn_w_qkv, attn_lambda_q1, attn_lambda_k1, attn_lambda_q2, attn_lambda_k2, attn_subln, attn_w_o):
    bp, tp, d = x_prompt.shape
    bs = x_sample.shape[0]
    depth = norm_ffn1.shape[0]
    n_even = even_w_in.shape[0]
    n_odd = attn_w_qkv.shape[0]

    w1_gu, w1_d, w2_gu, w2_d = (w.astype(BF16) for w in (w_ffn1_gu, w_ffn1_down, w_ffn2_gu, w_ffn2_down))
    a0 = 2 * GLA_QK + 2 * GLA_V
    w_in_main = jnp.concatenate([even_w_in[:, :, :a0], even_w_in[:, :, a0 + GLA_RANK:]], axis=-1).astype(BF16)
    w_in_alo = jnp.pad(even_w_in[:, :, a0:a0 + GLA_RANK], ((0, 0), (0, 0), (0, LANES - GLA_RANK))).astype(BF16)
    w_alpha = jnp.pad(gla_w_alpha, ((0, 0), (0, LANES - GLA_RANK), (0, 0))).astype(BF16)
    b_alpha = gla_b_alpha.reshape(n_even, 1, GLA_QK)
    out_gain = gla_out_norm.reshape(n_even, 1, GLA_DV)
    w_even_out = even_w_out.astype(BF16)
    w_qkv = attn_w_qkv.astype(BF16)
    w_o = attn_w_o.astype(BF16)
    lams = tuple(v.reshape(n_odd, 1, ATTN_HD) for v in (attn_lambda_q1, attn_lambda_k1, attn_lambda_q2, attn_lambda_k2))
    subln = attn_subln.reshape(n_odd, 1, ATTN_VD)
    q_scale = jnp.full((1, ATTN_QK), ATTN_SCALE, F32)
    bias_rows = jnp.tile(rel_bias.T, (2, 1))
    n_in = w_in_main.shape[-1]

    xp = x_prompt.reshape(bp * tp, d)
    xs = _pad_rows(x_sample.reshape(bs, d), SAMPLE_ROWS)
    conv_t = jnp.pad(jnp.swapaxes(state_conv, 1, 2), ((0, 0), (0, 0), (0, SAMPLE_ROWS - bs), (0, 0)))

    k_p, v_p, gla_p, conv_p = [], [], [], []
    k_s, v_s, gla_s, conv_s = [], [], [], []

    def mm_both(ap_list, as_list, w, layer, col0, n, out_dtypes, res=(None, None), scale=None):
        rp = matmul(ap_list, w, layer, col0, n, out_dtypes, PROMPT_TM, MM_TN, res=res[0], scale=scale)
        rs = matmul(as_list, w, layer, col0, n, out_dtypes, SAMPLE_ROWS, MM_TN, res=res[1], scale=scale)
        return rp, rs

    for l in range(depth):
        xp = half_ffn(xp, norm_ffn1[l], w1_gu, w1_d, l, PROMPT_TM)
        xs = half_ffn(xs, norm_ffn1[l], w1_gu, w1_d, l, SAMPLE_ROWS)
        hp = rmsnorm(xp, norm_mix[l], BF16)
        hs = rmsnorm(xs, norm_mix[l], BF16)
        if l % 2 == 0:
            e = l // 2
            (proj_p,), (proj_s,) = mm_both([hp], [hs], w_in_main, e, 0, n_in, (F32,))
            alo_p = matmul([hp], w_in_alo, e, 0, LANES, (F32,), PROMPT_TM, LANES)[0]
            alo_s = matmul([hs], w_in_alo, e, 0, LANES, (F32,), SAMPLE_ROWS, LANES)[0]
            og_p, st_p = gla_prompt(proj_p, alo_p, w_alpha, b_alpha, out_gain, e, bp, tp)
            oc_p, buf_p = conv_prompt(proj_p, short_conv_w, e, bp, tp)
            og_s, st_s = gla_sample(proj_s, alo_s, w_alpha, b_alpha, out_gain, state_gla, e, bs)
            oc_s, buf_s = conv_sample(proj_s, conv_t[e], short_conv_w, e)
            og_s = _pad_rows(og_s.reshape(bs, GLA_V), SAMPLE_ROWS).astype(BF16)
            (xp,), (xs,) = mm_both([og_p, oc_p], [og_s, oc_s.astype(BF16)], w_even_out, e, 0, d, (F32,),
                                   res=(xp, xs))
            gla_p.append(st_p)
            conv_p.append(buf_p)
            gla_s.append(st_s)
            conv_s.append(jnp.swapaxes(buf_s[:, :bs], 0, 1))
        else:
            a = l // 2
            lam_init = 0.8 - 0.6 * math.exp(-0.3 * l)
            (qb_p,), (qb_s,) = mm_both([hp], [hs], w_qkv, a, 0, ATTN_QK, (BF16,), scale=q_scale)
            (kf_p, kb_p), (kf_s, _) = mm_both([hp], [hs], w_qkv, a, ATTN_QK, ATTN_QK, (F32, BF16))
            (vf_p, vb_p), (vf_s, _) = mm_both([hp], [hs], w_qkv, a, 2 * ATTN_QK, ATTN_QK, (F32, BF16))
            o_p = attn_prompt(qb_p, kb_p, vb_p, rel_bias, lams, subln, a, lam_init, bp, tp)
            q4 = qb_s[:bs].reshape(bs, ATTN_HEADS, 2, ATTN_HD)
            z4 = jnp.zeros_like(q4[:, :, 0])
            qm = jnp.concatenate([jnp.concatenate([q4[:, :, 0], z4], axis=-1),
                                  jnp.concatenate([z4, q4[:, :, 1]], axis=-1)], axis=1)
            kn = kf_s[:bs].reshape(bs, ATTN_HEADS, 2 * ATTN_HD)
            vn = vf_s[:bs].reshape(bs, ATTN_HEADS, ATTN_VD)
            o_s = attn_sample(qm, cache_k, cache_v, kn, vn, page_table, bias_rows, lams, subln, a, lam_init)
            o_s = _pad_rows(o_s.reshape(bs, ATTN_HEADS * ATTN_VD), SAMPLE_ROWS).astype(BF16)
            (xp,), (xs,) = mm_both([o_p], [o_s], w_o, a, 0, d, (F32,), res=(xp, xs))
            k_p.append(kf_p.reshape(bp, tp, ATTN_HEADS, 2 * ATTN_HD))
            v_p.append(vf_p.reshape(bp, tp, ATTN_HEADS, ATTN_VD))
            k_s.append(kn.reshape(bs, 1, ATTN_HEADS, 2 * ATTN_HD))
            v_s.append(vn.reshape(bs, 1, ATTN_HEADS, ATTN_VD))
        xp = half_ffn(xp, norm_ffn2[l], w2_gu, w2_d, l, PROMPT_TM)
        xs = half_ffn(xs, norm_ffn2[l], w2_gu, w2_d, l, SAMPLE_ROWS)

    y_p = rmsnorm(xp, norm_final, F32).reshape(bp, tp, d)
    y_s = rmsnorm(xs, norm_final, F32)[:bs].reshape(bs, 1, d)
    return (y_p, y_s,
            jnp.stack(k_p), jnp.stack(v_p), jnp.stack(gla_p), jnp.stack(conv_p),
            jnp.stack(k_s), jnp.stack(v_s), jnp.stack(gla_s), jnp.stack(conv_s))
```

```python
import functools
import math

import numpy as np
import jax
import jax.numpy as jnp
from jax import lax
from jax.experimental import pallas as pl
from jax.experimental.pallas import tpu as pltpu

F32 = jnp.float32
BF16 = jnp.bfloat16

D_MODEL = 4096
D_FF = 11008
RMS_EPS = 1e-6
PAGE_SIZE = 128

GLA_HEADS = 4
GLA_DK = 256
GLA_DV = 512
GLA_RANK = 16
GLA_TAU = 16.0
GLA_CHUNK = 64
GLA_SUB = 16
GLA_QK = GLA_HEADS * GLA_DK
GLA_V = GLA_HEADS * GLA_DV
CONV_CH = 2048
CONV_W = 3

ATTN_HEADS = 16
ATTN_HD = 128
ATTN_VD = 256
ATTN_QK = ATTN_HEADS * 2 * ATTN_HD
ATTN_SCALE = ATTN_HD ** -0.5
NEG_INF = -1e30
NUM_BUCKETS = 32
MAX_DISTANCE = 128

LANES = 128
FFN_TF = 256
VMEM_LIMIT = 60 * 1024 * 1024


def _bucket_steps():
    max_exact = NUM_BUCKETS // 2
    table = []
    n = 0
    while True:
        if n < max_exact:
            b = n
        else:
            b = min(max_exact + int(math.log(n / max_exact) / math.log(MAX_DISTANCE / max_exact)
                                    * (NUM_BUCKETS - max_exact)), NUM_BUCKETS - 1)
        table.append(b)
        if b == NUM_BUCKETS - 1:
            break
        n += 1
    steps = [(i, table[i]) for i in range(1, len(table)) if table[i] != table[i - 1]]
    return table[0], steps


_BUCKET0, _BUCKET_STEPS = _bucket_steps()


def _dot(a, b):
    return jnp.dot(a, b, preferred_element_type=F32)


def _dot_nt(a, b):
    return lax.dot_general(a, b, (((1,), (1,)), ((), ())), preferred_element_type=F32)


def _dot_tn(a, b):
    return lax.dot_general(a, b, (((0,), (0,)), ((), ())), preferred_element_type=F32)


def _params(*sem):
    return pltpu.CompilerParams(dimension_semantics=sem, vmem_limit_bytes=VMEM_LIMIT)


def _silu(x):
    return x * (1.0 / (1.0 + jnp.exp(-x)))


def _rms_rows(x, gain):
    ms = jnp.mean(x * x, axis=-1, keepdims=True)
    return x * lax.rsqrt(ms + RMS_EPS) * gain


def _rmsnorm_body(x_ref, g_ref, o_ref):
    o_ref[...] = _rms_rows(x_ref[...], g_ref[...]).astype(o_ref.dtype)


def rmsnorm(x, gain, out_dtype, tm):
    m, d = x.shape
    return pl.pallas_call(
        _rmsnorm_body,
        grid=(m // tm,),
        in_specs=[pl.BlockSpec((tm, d), lambda i: (i, 0)),
                  pl.BlockSpec((1, d), lambda i: (0, 0))],
        out_specs=pl.BlockSpec((tm, d), lambda i: (i, 0)),
        out_shape=jax.ShapeDtypeStruct((m, d), out_dtype),
        compiler_params=_params("arbitrary"),
        name="rmsnorm",
    )(x, gain.reshape(1, d))


def rmsnorm_untile(x3, gain, row0, rows, tm):
    tiles, _, d = x3.shape
    nb, b0 = rows // tm, row0 // tm
    return pl.pallas_call(
        _rmsnorm_body,
        grid=(tiles, nb),
        in_specs=[pl.BlockSpec((None, tm, d), lambda i, s: (i, b0 + s, 0)),
                  pl.BlockSpec((1, d), lambda i, s: (0, 0))],
        out_specs=pl.BlockSpec((tm, d), lambda i, s: (i * nb + s, 0)),
        out_shape=jax.ShapeDtypeStruct((tiles * rows, d), F32),
        compiler_params=_params("arbitrary", "arbitrary"),
        name="rmsnorm_untile",
    )(x3, gain.reshape(1, d))


FFN_ROW_CHUNK = 80


def _ffn_body(x_ref, g_ref, wg_ref, wu_ref, wd_ref, o_ref, xn_ref, *, tm, nj):
    j = pl.program_id(1)
    rc = FFN_ROW_CHUNK

    @pl.when(j == 0)
    def _():
        def body(r, c):
            rows = pl.ds(pl.multiple_of(r * rc, rc), rc)
            xn_ref[rows, :] = _rms_rows(x_ref[rows, :], g_ref[...]).astype(BF16)
            o_ref[rows, :] = jnp.zeros((rc, o_ref.shape[1]), F32)
            return c
        lax.fori_loop(0, tm // rc, body, 0)

    xn = xn_ref[...]
    g = _dot(xn, wg_ref[...])
    u = _dot(xn, wu_ref[...])
    a = (_silu(g) * u).astype(BF16)
    o_ref[...] += _dot(a, wd_ref[...])

    @pl.when(j == nj - 1)
    def _():
        def body(r, c):
            rows = pl.ds(pl.multiple_of(r * rc, rc), rc)
            o_ref[rows, :] = x_ref[rows, :] + 0.5 * o_ref[rows, :]
            return c
        lax.fori_loop(0, tm // rc, body, 0)


def half_ffn(x, gain, w_gu, w_down, layer, tm):
    m, d = x.shape
    nj = D_FF // FFN_TF
    return pl.pallas_call(
        functools.partial(_ffn_body, tm=tm, nj=nj),
        grid=(m // tm, nj),
        in_specs=[pl.BlockSpec((tm, d), lambda i, j: (i, 0), pipeline_mode=pl.Buffered(1)),
                  pl.BlockSpec((1, d), lambda i, j: (0, 0)),
                  pl.BlockSpec((None, d, FFN_TF), lambda i, j: (layer, 0, j)),
                  pl.BlockSpec((None, d, FFN_TF), lambda i, j: (layer, 0, j + nj)),
                  pl.BlockSpec((None, FFN_TF, d), lambda i, j: (layer, j, 0))],
        out_specs=pl.BlockSpec((tm, d), lambda i, j: (i, 0), pipeline_mode=pl.Buffered(1)),
        out_shape=jax.ShapeDtypeStruct((m, d), F32),
        scratch_shapes=[pltpu.VMEM((tm, d), BF16)],
        compiler_params=_params("arbitrary", "arbitrary"),
        name="half_ffn",
    )(x, gain.reshape(1, d), w_gu, w_gu, w_down)


PROMPT_TM = 1024
SAMPLE_ROWS = 16
TILE_ROWS = PROMPT_TM + SAMPLE_ROWS
MM_TN = 512


def _proj_body(a_ref, w_ref, *refs, has_scale, n_out):
    scale_ref = refs[0] if has_scale else None
    out_refs = refs[1:] if has_scale else refs
    acc = _dot(a_ref[...], w_ref[...])
    for t in range(n_out):
        op_ref, os_ref = out_refs[2 * t], out_refs[2 * t + 1]
        val = acc * scale_ref[...] if (has_scale and op_ref.dtype == BF16) else acc
        op_ref[...] = val[:PROMPT_TM].astype(op_ref.dtype)
        os_ref[...] = val[PROMPT_TM:].astype(os_ref.dtype)


def proj_matmul(a, w, layer, col0, n, out_dtypes, scale=None):
    tiles = a.shape[0] // TILE_ROWS
    k_total = w.shape[1]
    tn = min(MM_TN, n)
    joff = col0 // tn
    in_specs = [pl.BlockSpec((TILE_ROWS, k_total), lambda i, j: (i, 0)),
                pl.BlockSpec((None, k_total, tn), lambda i, j: (layer, 0, j + joff))]
    args = [a, w]
    if scale is not None:
        in_specs.append(pl.BlockSpec((1, tn), lambda i, j: (0, j)))
        args.append(scale)
    out_specs, out_shape = [], []
    for dt in out_dtypes:
        out_specs += [pl.BlockSpec((PROMPT_TM, tn), lambda i, j: (i, j)),
                      pl.BlockSpec((None, SAMPLE_ROWS, tn), lambda i, j: (i, 0, j))]
        out_shape += [jax.ShapeDtypeStruct((tiles * PROMPT_TM, n), dt),
                      jax.ShapeDtypeStruct((tiles, SAMPLE_ROWS, n), dt)]
    outs = pl.pallas_call(
        functools.partial(_proj_body, has_scale=scale is not None, n_out=len(out_dtypes)),
        grid=(tiles, n // tn),
        in_specs=in_specs,
        out_specs=out_specs,
        out_shape=out_shape,
        compiler_params=_params("arbitrary", "arbitrary"),
        name="proj_matmul",
    )(*args)
    return [(outs[2 * t], outs[2 * t + 1][0]) for t in range(len(out_dtypes))]


def _out_proj_body(*refs, n_a):
    ap_refs, as_refs = refs[:n_a], refs[n_a:2 * n_a]
    w_ref, res_ref, o_ref, a_scr = refs[2 * n_a:]

    @pl.when(pl.program_id(1) == 0)
    def _():
        k0 = 0
        for ap_ref, as_ref in zip(ap_refs, as_refs):
            kk = ap_ref.shape[1]
            a_scr[0:PROMPT_TM, k0:k0 + kk] = ap_ref[...]
            a_scr[PROMPT_TM:TILE_ROWS, k0:k0 + kk] = as_ref[...]
            k0 += kk

    o_ref[...] = res_ref[...] + _dot(a_scr[...], w_ref[...])


def out_proj_matmul(ap_list, as_list, w, layer, res):
    m, n = res.shape
    k_total = w.shape[1]
    tn = MM_TN
    in_specs = ([pl.BlockSpec((PROMPT_TM, a.shape[1]), lambda i, j: (i, 0)) for a in ap_list]
                + [pl.BlockSpec((SAMPLE_ROWS, a.shape[1]), lambda i, j: (0, 0)) for a in as_list]
                + [pl.BlockSpec((None, k_total, tn), lambda i, j: (layer, 0, j)),
                   pl.BlockSpec((TILE_ROWS, tn), lambda i, j: (i, j))])
    return pl.pallas_call(
        functools.partial(_out_proj_body, n_a=len(ap_list)),
        grid=(m // TILE_ROWS, n // tn),
        in_specs=in_specs,
        out_specs=pl.BlockSpec((TILE_ROWS, tn), lambda i, j: (i, j)),
        out_shape=jax.ShapeDtypeStruct((m, n), F32),
        scratch_shapes=[pltpu.VMEM((TILE_ROWS, k_total), BF16)],
        compiler_params=_params("arbitrary", "arbitrary"),
        name="out_proj_matmul",
    )(*ap_list, *as_list, w, res)


def _log_decay(alo, walpha, balpha):
    z = _dot(alo.astype(BF16), walpha) + balpha
    return -(jnp.maximum(-z, 0.0) + jnp.log1p(jnp.exp(-jnp.abs(z)))) / GLA_TAU


def _cumsum_rows(x):
    n = x.shape[0]
    rows = lax.broadcasted_iota(jnp.int32, x.shape, 0)
    s = 1
    while s < n:
        x = x + jnp.where(rows >= s, pltpu.roll(x, s, 0), 0.0)
        s *= 2
    return x


def _gla_chunk_head(q, k, v, r, cum, gain, st):
    C, SUB = GLA_CHUNK, GLA_SUB
    vb = v.astype(BF16)
    last = cum[C - 1:C, :]

    inter = _dot_nt((q * jnp.exp(cum)).astype(BF16), st.astype(BF16))

    rows_c = lax.broadcasted_iota(jnp.int32, (C, GLA_DK), 0)
    lane_c = lax.broadcasted_iota(jnp.int32, (SUB, C), 1)
    sub_j = lax.broadcasted_iota(jnp.int32, (SUB, GLA_DK), 0)
    diag_blocks = []
    att_t = jnp.zeros((C, C), F32)
    for blk in range(C // SUB):
        lo = blk * SUB
        q_b, k_b, c_b = q[lo:lo + SUB], k[lo:lo + SUB], cum[lo:lo + SUB]
        dg = jnp.zeros((SUB, C), F32)
        for i in range(SUB):
            rel = c_b[i:i + 1, :] - c_b
            dec = jnp.exp(jnp.where(sub_j <= i, rel, -jnp.inf))
            col = jnp.sum(q_b[i:i + 1, :] * k_b * dec, axis=-1, keepdims=True)
            dg = jnp.where(lane_c == lo + i, col, dg)
        diag_blocks.append(dg)
        if blk > 0:
            edge = cum[lo - 1:lo, :]
            in_blk = (rows_c >= lo) & (rows_c < lo + SUB)
            q_t = jnp.where(in_blk, q * jnp.exp(jnp.where(in_blk, cum - edge, 0.0)), 0.0)
            k_t = k[:lo] * jnp.exp(edge - cum[:lo])
            off = _dot_nt(k_t.astype(BF16), q_t.astype(BF16))
            att_t = att_t + jnp.concatenate([off, jnp.zeros((C - lo, C), F32)], axis=0)
    att_t = att_t + jnp.concatenate(diag_blocks, axis=0)

    o = inter + _dot_tn(att_t.astype(BF16), vb)
    y = _rms_rows(o, gain) * _silu(r)

    k_l = (k * jnp.exp(last - cum)).astype(BF16)
    st_new = st * jnp.exp(last) + _dot_tn(vb, k_l)
    return y, st_new


def _gla_chunk_body(q_ref, k_ref, v_ref, r_ref, alo_ref, wa_ref, ba_ref, gain_ref, o_ref, s_ref, st_ref):
    c = pl.program_id(1)
    nc = pl.num_programs(1)

    @pl.when(c == 0)
    def _():
        st_ref[...] = jnp.zeros_like(st_ref)

    cum = _cumsum_rows(_log_decay(alo_ref[...], wa_ref[...], ba_ref[...]))
    for h in range(GLA_HEADS):
        dk = slice(h * GLA_DK, (h + 1) * GLA_DK)
        dv = slice(h * GLA_DV, (h + 1) * GLA_DV)
        y, st_new = _gla_chunk_head(q_ref[:, dk] * GLA_DK ** -0.5, k_ref[:, dk], v_ref[:, dv], r_ref[:, dv],
                                    cum[:, dk], gain_ref[...], st_ref[h])
        o_ref[:, dv] = y.astype(o_ref.dtype)
        st_ref[h] = st_new

    @pl.when(c == nc - 1)
    def _():
        for h in range(GLA_HEADS):
            s_ref[h] = st_ref[h].T


def gla_prompt(proj, alo, w_alpha, b_alpha, out_gain, e, batch, seq):
    m = proj.shape[0]
    nc = seq // GLA_CHUNK
    C = GLA_CHUNK
    row = lambda b, c: b * nc + c
    return pl.pallas_call(
        _gla_chunk_body,
        grid=(batch, nc),
        in_specs=[pl.BlockSpec((C, GLA_QK), lambda b, c: (row(b, c), 0)),
                  pl.BlockSpec((C, GLA_QK), lambda b, c: (row(b, c), 1)),
                  pl.BlockSpec((C, GLA_V), lambda b, c: (row(b, c), 1)),
                  pl.BlockSpec((C, GLA_V), lambda b, c: (row(b, c), 2)),
                  pl.BlockSpec((C, LANES), lambda b, c: (row(b, c), 0)),
                  pl.BlockSpec((None, LANES, GLA_QK), lambda b, c: (e, 0, 0)),
                  pl.BlockSpec((None, 1, GLA_QK), lambda b, c: (e, 0, 0)),
                  pl.BlockSpec((None, 1, GLA_DV), lambda b, c: (e, 0, 0))],
        out_specs=[pl.BlockSpec((C, GLA_V), lambda b, c: (row(b, c), 0)),
                   pl.BlockSpec((None, GLA_HEADS, GLA_DK, GLA_DV), lambda b, c: (b, 0, 0, 0))],
        out_shape=[jax.ShapeDtypeStruct((m, GLA_V), BF16),
                   jax.ShapeDtypeStruct((batch, GLA_HEADS, GLA_DK, GLA_DV), F32)],
        scratch_shapes=[pltpu.VMEM((GLA_HEADS, GLA_DV, GLA_DK), F32)],
        compiler_params=_params("arbitrary", "arbitrary"),
        name="gla_prompt",
    )(proj, proj, proj, proj, alo, w_alpha, b_alpha, out_gain)


def _row_to_col(row, n):
    r = lax.broadcasted_iota(jnp.int32, (n, n), 0)
    c = lax.broadcasted_iota(jnp.int32, (n, n), 1)
    return jnp.sum(jnp.where(r == c, row, 0.0), axis=1, keepdims=True)


def _gla_step_body(q_ref, k_ref, v_ref, r_ref, alo_ref, wa_ref, ba_ref, gain_ref, s0_ref, o_ref, s_ref):
    b = pl.program_id(0)

    def pick(ref):
        blk = ref[...]
        rows = lax.broadcasted_iota(jnp.int32, blk.shape, 0)
        return jnp.sum(jnp.where(rows == b, blk, 0.0), axis=0, keepdims=True)

    q = pick(q_ref) * GLA_DK ** -0.5
    k = pick(k_ref)
    v = pick(v_ref)
    r = pick(r_ref)
    g = _log_decay(pick(alo_ref), wa_ref[...], ba_ref[...])
    a_col = _row_to_col(jnp.exp(g), GLA_DK)
    k_col = _row_to_col(k, GLA_DK)
    q_col = _row_to_col(q, GLA_DK)
    s_new = a_col * s0_ref[...] + k_col * v
    s_ref[...] = s_new
    o = jnp.sum(q_col * s_new, axis=0, keepdims=True)
    o_ref[...] = _rms_rows(o, gain_ref[...]) * _silu(r)


def gla_sample(proj, alo, w_alpha, b_alpha, out_gain, state, e, batch):
    rows = proj.shape[0]
    return pl.pallas_call(
        _gla_step_body,
        grid=(batch, GLA_HEADS),
        in_specs=[pl.BlockSpec((rows, GLA_DK), lambda b, h: (0, h)),
                  pl.BlockSpec((rows, GLA_DK), lambda b, h: (0, GLA_HEADS + h)),
                  pl.BlockSpec((rows, GLA_DV), lambda b, h: (0, GLA_HEADS + h)),
                  pl.BlockSpec((rows, GLA_DV), lambda b, h: (0, 2 * GLA_HEADS + h)),
                  pl.BlockSpec((rows, LANES), lambda b, h: (0, 0)),
                  pl.BlockSpec((None, LANES, GLA_DK), lambda b, h: (e, 0, h)),
                  pl.BlockSpec((None, 1, GLA_DK), lambda b, h: (e, 0, h)),
                  pl.BlockSpec((None, 1, GLA_DV), lambda b, h: (e, 0, 0)),
                  pl.BlockSpec((None, None, None, GLA_DK, GLA_DV), lambda b, h: (e, b, h, 0, 0))],
        out_specs=[pl.BlockSpec((None, 1, GLA_DV), lambda b, h: (b, 0, h)),
                   pl.BlockSpec((None, None, GLA_DK, GLA_DV), lambda b, h: (b, h, 0, 0))],
        out_shape=[jax.ShapeDtypeStruct((batch, 1, GLA_V), F32),
                   jax.ShapeDtypeStruct((batch, GLA_HEADS, GLA_DK, GLA_DV), F32)],
        compiler_params=_params("arbitrary", "arbitrary"),
        name="gla_sample",
    )(proj, proj, proj, proj, alo, w_alpha, b_alpha, out_gain, state)


CONV_TT = 256
CONV_HALO = 8


def _conv_prompt_body(gb_ref, gc_ref, hin_ref, pc_ref, ph_ref, w_ref, y_ref, buf_ref, u_ref):
    i = pl.program_id(1)
    nt = pl.num_programs(1)
    T, H = CONV_TT, CONV_HALO
    u = gc_ref[...] * hin_ref[...]
    prev = pc_ref[...] * ph_ref[...]
    u_ref[0:H, :] = jnp.where(i > 0, prev, 0.0)
    u_ref[H:H + T, :] = u
    w = w_ref[...]
    conv = w[0:1, :] * u_ref[H - 2:H - 2 + T, :] + w[1:2, :] * u_ref[H - 1:H - 1 + T, :] + w[2:3, :] * u
    y_ref[...] = (gb_ref[...] * conv).astype(y_ref.dtype)

    @pl.when(i == nt - 1)
    def _():
        buf_ref[...] = u_ref[H + T - (CONV_W - 1):H + T, :]


def conv_prompt(proj, conv_w, e, batch, seq):
    m = proj.shape[0]
    T, H = CONV_TT, CONV_HALO
    nt = seq // T
    row = lambda b, i: b * nt + i
    prev = lambda b, i: jnp.maximum((b * seq + i * T) // H - 1, 0)
    return pl.pallas_call(
        _conv_prompt_body,
        grid=(batch, nt),
        in_specs=[pl.BlockSpec((T, CONV_CH), lambda b, i: (row(b, i), 3)),
                  pl.BlockSpec((T, CONV_CH), lambda b, i: (row(b, i), 4)),
                  pl.BlockSpec((T, CONV_CH), lambda b, i: (row(b, i), 5)),
                  pl.BlockSpec((H, CONV_CH), lambda b, i: (prev(b, i), 4)),
                  pl.BlockSpec((H, CONV_CH), lambda b, i: (prev(b, i), 5)),
                  pl.BlockSpec((None, CONV_W, CONV_CH), lambda b, i: (e, 0, 0))],
        out_specs=[pl.BlockSpec((T, CONV_CH), lambda b, i: (row(b, i), 0)),
                   pl.BlockSpec((None, CONV_W - 1, CONV_CH), lambda b, i: (b, 0, 0))],
        out_shape=[jax.ShapeDtypeStruct((m, CONV_CH), BF16),
                   jax.ShapeDtypeStruct((batch, CONV_W - 1, CONV_CH), F32)],
        scratch_shapes=[pltpu.VMEM((H + T, CONV_CH), F32)],
        compiler_params=_params("arbitrary", "arbitrary"),
        name="conv_prompt",
    )(proj, proj, proj, proj, proj, conv_w)


def _conv_sample_body(gb_ref, gc_ref, hin_ref, buf_ref, w_ref, y_ref, nbuf_ref):
    u = gc_ref[...] * hin_ref[...]
    w = w_ref[...]
    b0 = buf_ref[0]
    b1 = buf_ref[1]
    y_ref[...] = gb_ref[...] * (w[0:1, :] * b0 + w[1:2, :] * b1 + w[2:3, :] * u)
    nbuf_ref[0] = b1
    nbuf_ref[1] = u


def conv_sample(proj, buf_t, conv_w, e):
    rows = proj.shape[0]
    return pl.pallas_call(
        _conv_sample_body,
        grid=(1,),
        in_specs=[pl.BlockSpec((rows, CONV_CH), lambda i: (0, 3)),
                  pl.BlockSpec((rows, CONV_CH), lambda i: (0, 4)),
                  pl.BlockSpec((rows, CONV_CH), lambda i: (0, 5)),
                  pl.BlockSpec((CONV_W - 1, rows, CONV_CH), lambda i: (0, 0, 0)),
                  pl.BlockSpec((None, CONV_W, CONV_CH), lambda i: (e, 0, 0))],
        out_specs=[pl.BlockSpec((rows, CONV_CH), lambda i: (0, 0)),
                   pl.BlockSpec((CONV_W - 1, rows, CONV_CH), lambda i: (0, 0, 0))],
        out_shape=[jax.ShapeDtypeStruct((rows, CONV_CH), F32),
                   jax.ShapeDtypeStruct((CONV_W - 1, rows, CONV_CH), F32)],
        compiler_params=_params("arbitrary"),
        name="conv_sample",
    )(proj, proj, proj, buf_t, conv_w)


def _bias_of_distance(n, bias_at):
    t = bias_at(_BUCKET0)
    for start, bucket in _BUCKET_STEPS:
        t = jnp.where(n >= start, bias_at(bucket), t)
    return t


def _lambda_full(lq1_ref, lk1_ref, lq2_ref, lk2_ref, lam_init):
    s1 = jnp.sum(lq1_ref[...] * lk1_ref[...], axis=-1, keepdims=True)
    s2 = jnp.sum(lq2_ref[...] * lk2_ref[...], axis=-1, keepdims=True)
    return jnp.exp(s1) - jnp.exp(s2) + lam_init


ATT_T = 256


def _attn_prompt_body(bias_ref, q_ref, k_ref, v_ref, lq1_ref, lk1_ref, lq2_ref, lk2_ref, sub_ref, o_ref,
                      tb_ref, *, lam_init, nq):
    h = pl.program_id(0)
    b = pl.program_id(1)
    qi = pl.program_id(2)
    T = ATT_T

    @pl.when((b == 0) & (qi == 0))
    def _():
        n0 = lax.broadcasted_iota(jnp.int32, (T, T), 0) - lax.broadcasted_iota(jnp.int32, (T, T), 1)
        bias_at = lambda bucket: bias_ref[bucket, h]
        tb_ref[0] = jnp.where(n0 >= 0, _bias_of_distance(n0, bias_at), NEG_INF)
        tb_ref[1] = _bias_of_distance(n0 + T, bias_at)

    q = q_ref[...]
    far_bias = bias_ref[NUM_BUCKETS - 1, h]
    lam = _lambda_full(lq1_ref, lk1_ref, lq2_ref, lk2_ref, lam_init)

    def one_map(c, n):
        cols = slice(c * ATTN_HD, (c + 1) * ATTN_HD)
        s = _dot_nt(q[:, cols], k_ref[0:(n + 1) * T, cols])
        far = [s[:, t * T:(t + 1) * T] for t in range(n - 1)]
        near = [s[:, t * T:(t + 1) * T] + tb_ref[n - t] for t in range(max(n - 1, 0), n + 1)]
        mx = near[0]
        for x in near[1:]:
            mx = jnp.maximum(mx, x)
        m = jnp.max(mx, axis=-1, keepdims=True)
        if far:
            fx = far[0]
            for x in far[1:]:
                fx = jnp.maximum(fx, x)
            m = jnp.maximum(m, jnp.max(fx, axis=-1, keepdims=True) + far_bias)
        m_far = m - far_bias
        ps = [jnp.exp(x - m_far) for x in far] + [jnp.exp(x - m) for x in near]
        tot = ps[0]
        for x in ps[1:]:
            tot = tot + x
        l = jnp.sum(tot, axis=-1, keepdims=True)
        p_all = jnp.concatenate([x.astype(BF16) for x in ps], axis=1)
        return _dot(p_all, v_ref[0:(n + 1) * T, :]) * (1.0 / l)

    for n in range(nq):
        @pl.when(qi == n)
        def _():
            o = one_map(0, n) - lam * one_map(1, n)
            o_ref[...] = (_rms_rows(o, sub_ref[...]) * (1.0 - lam_init)).astype(o_ref.dtype)


def attn_prompt(qb, kb, vb, rel_bias, lams, subln, a, lam_init, batch, seq):
    m = qb.shape[0]
    T = ATT_T
    nq = seq // T
    lam_spec = pl.BlockSpec((None, 1, ATTN_HD), lambda h, b, i: (a, 0, 0))
    return pl.pallas_call(
        functools.partial(_attn_prompt_body, lam_init=lam_init, nq=nq),
        grid=(ATTN_HEADS, batch, nq),
        in_specs=[pl.BlockSpec(memory_space=pltpu.SMEM),
                  pl.BlockSpec((T, ATTN_VD), lambda h, b, i: (b * nq + i, h)),
                  pl.BlockSpec((seq, ATTN_VD), lambda h, b, i: (b, h)),
                  pl.BlockSpec((seq, ATTN_VD), lambda h, b, i: (b, h)),
                  lam_spec, lam_spec, lam_spec, lam_spec,
                  pl.BlockSpec((None, 1, ATTN_VD), lambda h, b, i: (a, 0, 0))],
        out_specs=pl.BlockSpec((T, ATTN_VD), lambda h, b, i: (b * nq + i, h)),
        out_shape=jax.ShapeDtypeStruct((m, ATTN_HEADS * ATTN_VD), BF16),
        scratch_shapes=[pltpu.VMEM((2, T, T), F32)],
        compiler_params=_params("arbitrary", "arbitrary", "arbitrary"),
        name="attn_prompt",
    )(rel_bias, qb, kb, vb, *lams, subln)


SAMPLE_PAGES_PER_STEP = 4


def _attn_sample_body(pt_ref, qm_ref, *refs, lam_init, n_groups):
    G = SAMPLE_PAGES_PER_STEP
    k_refs, v_refs = refs[:G], refs[G:2 * G]
    (kn_ref, vn_ref, bias_ref, lq1_ref, lk1_ref, lq2_ref, lk2_ref, sub_ref, o_ref, m_ref, l_ref, acc_ref) = refs[2 * G:]
    g = pl.program_id(1)
    H = ATTN_HEADS
    R = 2 * H
    P = PAGE_SIZE
    qm = qm_ref[...]

    @pl.when(g == 0)
    def _():
        m_ref[...] = jnp.full(m_ref.shape, -jnp.inf, F32)
        l_ref[...] = jnp.zeros_like(l_ref)
        acc_ref[...] = jnp.zeros_like(acc_ref)

    def update(k_list, v_list, bias_list):
        n = k_list[0].shape[0]
        own = (lax.broadcasted_iota(jnp.int32, (R, n), 0) % H) == (lax.broadcasted_iota(jnp.int32, (R, n), 1) % H)
        s_list = [jnp.where(own, _dot_nt(qm, k2d) + bias, NEG_INF) for k2d, bias in zip(k_list, bias_list)]
        m_old = m_ref[...]
        m_new = m_old
        for s in s_list:
            m_new = jnp.maximum(m_new, jnp.max(s, axis=-1, keepdims=True))
        alpha = jnp.exp(m_old - m_new)
        l_new = alpha * l_ref[...]
        acc_new = alpha * acc_ref[...]
        for s, v2d in zip(s_list, v_list):
            p = jnp.exp(s - m_new)
            l_new = l_new + jnp.sum(p, axis=-1, keepdims=True)
            acc_new = acc_new + _dot(p.astype(BF16), v2d)
        m_ref[...] = m_new
        l_ref[...] = l_new
        acc_ref[...] = acc_new

    def pages(bias_list):
        k_list = [r[...].reshape(P * H, 2 * ATTN_HD).astype(BF16) for r in k_refs]
        v_list = [r[...].reshape(P * H, ATTN_VD).astype(BF16) for r in v_refs]
        update(k_list, v_list, bias_list)

    far_bias = bias_ref[:, NUM_BUCKETS - 1:NUM_BUCKETS]

    @pl.when(g < n_groups - 1)
    def _():
        pages([far_bias] * G)

    @pl.when(g == n_groups - 1)
    def _():
        dist = P - lax.broadcasted_iota(jnp.int32, (R, P * H), 1) // H
        near = _bias_of_distance(dist, lambda bucket: bias_ref[:, bucket:bucket + 1])
        pages([far_bias] * (G - 1) + [near])

    @pl.when(g == n_groups)
    def _():
        update([kn_ref[...].astype(BF16)], [vn_ref[...].astype(BF16)], [bias_ref[:, 0:1]])
        lam = _lambda_full(lq1_ref, lk1_ref, lq2_ref, lk2_ref, lam_init)
        on = acc_ref[...] * (1.0 / l_ref[...])
        o = on[:H] - lam * on[H:]
        o_ref[...] = _rms_rows(o, sub_ref[...]) * (1.0 - lam_init)


def attn_sample(qm, cache_k, cache_v, k_new, v_new, page_table, bias_rows, lams, subln, a, lam_init):
    batch, n_pages = page_table.shape
    H, P, G = ATTN_HEADS, PAGE_SIZE, SAMPLE_PAGES_PER_STEP
    R = 2 * H
    n_groups = n_pages // G

    def page_spec(t, width):
        return pl.BlockSpec((None, None, P, H, width),
                            lambda b, g, pt: (a, pt[b, jnp.minimum(g * G + t, n_pages - 1)], 0, 0, 0))

    lam_spec = pl.BlockSpec((None, 1, ATTN_HD), lambda b, g, pt: (a, 0, 0))
    grid_spec = pltpu.PrefetchScalarGridSpec(
        num_scalar_prefetch=1,
        grid=(batch, n_groups + 1),
        in_specs=([pl.BlockSpec((None, R, 2 * ATTN_HD), lambda b, g, pt: (b, 0, 0))]
                  + [page_spec(t, 2 * ATTN_HD) for t in range(G)]
                  + [page_spec(t, ATTN_VD) for t in range(G)]
                  + [pl.BlockSpec((None, H, 2 * ATTN_HD), lambda b, g, pt: (b, 0, 0)),
                     pl.BlockSpec((None, H, ATTN_VD), lambda b, g, pt: (b, 0, 0)),
                     pl.BlockSpec((R, NUM_BUCKETS), lambda b, g, pt: (0, 0)),
                     lam_spec, lam_spec, lam_spec, lam_spec,
                     pl.BlockSpec((None, 1, ATTN_VD), lambda b, g, pt: (a, 0, 0))]),
        out_specs=pl.BlockSpec((None, H, ATTN_VD), lambda b, g, pt: (b, 0, 0)),
        scratch_shapes=[pltpu.VMEM((R, 1), F32), pltpu.VMEM((R, 1), F32), pltpu.VMEM((R, ATTN_VD), F32)],
    )
    return pl.pallas_call(
        functools.partial(_attn_sample_body, lam_init=lam_init, n_groups=n_groups),
        grid_spec=grid_spec,
        out_shape=jax.ShapeDtypeStruct((batch, H, ATTN_VD), F32),
        compiler_params=_params("arbitrary", "arbitrary"),
        name="attn_sample",
    )(page_table, qm, *([cache_k] * G), *([cache_v] * G), k_new, v_new, bias_rows, *lams, subln)


NORM_TM = 208


def _pad_rows(x, rows):
    return jnp.pad(x, ((0, rows - x.shape[0]), (0, 0)))


def kernel(x_prompt, x_sample, cache_k, cache_v, state_gla, state_conv, page_table, rel_bias, norm_ffn1, w_ffn1_gu, w_ffn1_down, norm_mix, norm_ffn2, w_ffn2_gu, w_ffn2_down, norm_final, even_w_in, gla_w_alpha, gla_b_alpha, gla_out_norm, short_conv_w, even_w_out, attn_w_qkv, attn_lambda_q1, attn_lambda_k1, attn_lambda_q2, attn_lambda_k2, attn_subln, attn_w_o):
    bp, tp, d = x_prompt.shape
    bs = x_sample.shape[0]
    depth = norm_ffn1.shape[0]
    n_even = even_w_in.shape[0]
    n_odd = attn_w_qkv.shape[0]

    w1_gu, w1_d, w2_gu, w2_d = (w.astype(BF16) for w in (w_ffn1_gu, w_ffn1_down, w_ffn2_gu, w_ffn2_down))
    a0 = 2 * GLA_QK + 2 * GLA_V
    w_in_main = jnp.concatenate([even_w_in[:, :, :a0], even_w_in[:, :, a0 + GLA_RANK:]], axis=-1).astype(BF16)
    w_in_alo = jnp.pad(even_w_in[:, :, a0:a0 + GLA_RANK], ((0, 0), (0, 0), (0, LANES - GLA_RANK))).astype(BF16)
    w_alpha = jnp.pad(gla_w_alpha, ((0, 0), (0, LANES - GLA_RANK), (0, 0))).astype(BF16)
    b_alpha = gla_b_alpha.reshape(n_even, 1, GLA_QK)
    out_gain = gla_out_norm.reshape(n_even, 1, GLA_DV)
    w_even_out = even_w_out.astype(BF16)
    w_qkv = attn_w_qkv.astype(BF16)
    w_o = attn_w_o.astype(BF16)
    lams = tuple(v.reshape(n_odd, 1, ATTN_HD) for v in (attn_lambda_q1, attn_lambda_k1, attn_lambda_q2, attn_lambda_k2))
    subln = attn_subln.reshape(n_odd, 1, ATTN_VD)
    q_scale = jnp.full((1, ATTN_QK), ATTN_SCALE, F32)
    bias_rows = jnp.tile(rel_bias.T, (2, 1))
    n_in = w_in_main.shape[-1]

    tiles = (bp * tp) // PROMPT_TM
    xs_pad = _pad_rows(x_sample.reshape(bs, d), SAMPLE_ROWS)
    x = jnp.concatenate([x_prompt.reshape(tiles, PROMPT_TM, d),
                         jnp.broadcast_to(xs_pad[None], (tiles, SAMPLE_ROWS, d))], axis=1).reshape(tiles * TILE_ROWS, d)
    conv_t = jnp.pad(jnp.swapaxes(state_conv, 1, 2), ((0, 0), (0, 0), (0, SAMPLE_ROWS - bs), (0, 0)))

    k_p, v_p, gla_p, conv_p = [], [], [], []
    k_s, v_s, gla_s, conv_s = [], [], [], []

    for l in range(depth):
        x = half_ffn(x, norm_ffn1[l], w1_gu, w1_d, l, TILE_ROWS)
        hn = rmsnorm(x, norm_mix[l], BF16, NORM_TM)
        if l % 2 == 0:
            e = l // 2
            ((proj_p, proj_s),) = proj_matmul(hn, w_in_main, e, 0, n_in, (F32,))
            ((alo_p, alo_s),) = proj_matmul(hn, w_in_alo, e, 0, LANES, (F32,))
            og_p, st_p = gla_prompt(proj_p, alo_p, w_alpha, b_alpha, out_gain, e, bp, tp)
            oc_p, buf_p = conv_prompt(proj_p, short_conv_w, e, bp, tp)
            og_s, st_s = gla_sample(proj_s, alo_s, w_alpha, b_alpha, out_gain, state_gla, e, bs)
            oc_s, buf_s = conv_sample(proj_s, conv_t[e], short_conv_w, e)
            og_s = _pad_rows(og_s.reshape(bs, GLA_V), SAMPLE_ROWS).astype(BF16)
            x = out_proj_matmul([og_p, oc_p], [og_s, oc_s.astype(BF16)], w_even_out, e, x)
            gla_p.append(st_p)
            conv_p.append(buf_p)
            gla_s.append(st_s)
            conv_s.append(jnp.swapaxes(buf_s[:, :bs], 0, 1))
        else:
            a = l // 2
            lam_init = 0.8 - 0.6 * math.exp(-0.3 * l)
            ((qb_p, qb_s),) = proj_matmul(hn, w_qkv, a, 0, ATTN_QK, (BF16,), scale=q_scale)
            (kf_p, kf_s), (kb_p, _) = proj_matmul(hn, w_qkv, a, ATTN_QK, ATTN_QK, (F32, BF16))
            (vf_p, vf_s), (vb_p, _) = proj_matmul(hn, w_qkv, a, 2 * ATTN_QK, ATTN_QK, (F32, BF16))
            o_p = attn_prompt(qb_p, kb_p, vb_p, rel_bias, lams, subln, a, lam_init, bp, tp)
            q4 = qb_s[:bs].reshape(bs, ATTN_HEADS, 2, ATTN_HD)
            z4 = jnp.zeros_like(q4[:, :, 0])
            qm = jnp.concatenate([jnp.concatenate([q4[:, :, 0], z4], axis=-1),
                                  jnp.concatenate([z4, q4[:, :, 1]], axis=-1)], axis=1)
            kn = kf_s[:bs].reshape(bs, ATTN_HEADS, 2 * ATTN_HD)
            vn = vf_s[:bs].reshape(bs, ATTN_HEADS, ATTN_VD)
            o_s = attn_sample(qm, cache_k, cache_v, kn, vn, page_table, bias_rows, lams, subln, a, lam_init)
            o_s = _pad_rows(o_s.reshape(bs, ATTN_HEADS * ATTN_VD), SAMPLE_ROWS).astype(BF16)
            x = out_proj_matmul([o_p], [o_s], w_o, a, x)
            k_p.append(kf_p.reshape(bp, tp, ATTN_HEADS, 2 * ATTN_HD))
            v_p.append(vf_p.reshape(bp, tp, ATTN_HEADS, ATTN_VD))
            k_s.append(kn.reshape(bs, 1, ATTN_HEADS, 2 * ATTN_HD))
            v_s.append(vn.reshape(bs, 1, ATTN_HEADS, ATTN_VD))
        x = half_ffn(x, norm_ffn2[l], w2_gu, w2_d, l, TILE_ROWS)

    x3 = x.reshape(tiles, TILE_ROWS, d)
    y_p = rmsnorm_untile(x3, norm_final, 0, PROMPT_TM, 256).reshape(bp, tp, d)
    y_s = rmsnorm_untile(x3[:1], norm_final, PROMPT_TM, SAMPLE_ROWS, SAMPLE_ROWS)[:bs].reshape(bs, 1, d)
    return (y_p, y_s,
            jnp.stack(k_p), jnp.stack(v_p), jnp.stack(gla_p), jnp.stack(conv_p),
            jnp.stack(k_s), jnp.stack(v_s), jnp.stack(gla_s), jnp.stack(conv_s))
```

```python
import functools
import math

import numpy as np
import jax
import jax.numpy as jnp
from jax import lax
from jax.experimental import pallas as pl
from jax.experimental.pallas import tpu as pltpu

F32 = jnp.float32
BF16 = jnp.bfloat16

D_MODEL = 4096
D_FF = 11008
RMS_EPS = 1e-6
PAGE_SIZE = 128

GLA_HEADS = 4
GLA_DK = 256
GLA_DV = 512
GLA_RANK = 16
GLA_TAU = 16.0
GLA_CHUNK = 64
GLA_SUB = 16
GLA_QK = GLA_HEADS * GLA_DK
GLA_V = GLA_HEADS * GLA_DV
CONV_CH = 2048
CONV_W = 3

ATTN_HEADS = 16
ATTN_HD = 128
ATTN_VD = 256
ATTN_QK = ATTN_HEADS * 2 * ATTN_HD
ATTN_SCALE = ATTN_HD ** -0.5
NEG_INF = -1e30
NUM_BUCKETS = 32
MAX_DISTANCE = 128

LANES = 128
FFN_TF = 256
VMEM_LIMIT = 60 * 1024 * 1024


def _bucket_steps():
    max_exact = NUM_BUCKETS // 2
    table = []
    n = 0
    while True:
        if n < max_exact:
            b = n
        else:
            b = min(max_exact + int(math.log(n / max_exact) / math.log(MAX_DISTANCE / max_exact)
                                    * (NUM_BUCKETS - max_exact)), NUM_BUCKETS - 1)
        table.append(b)
        if b == NUM_BUCKETS - 1:
            break
        n += 1
    steps = [(i, table[i]) for i in range(1, len(table)) if table[i] != table[i - 1]]
    return table[0], steps


_BUCKET0, _BUCKET_STEPS = _bucket_steps()


def _dot(a, b):
    return jnp.dot(a, b, preferred_element_type=F32)


def _dot_nt(a, b):
    return lax.dot_general(a, b, (((1,), (1,)), ((), ())), preferred_element_type=F32)


def _dot_tn(a, b):
    return lax.dot_general(a, b, (((0,), (0,)), ((), ())), preferred_element_type=F32)


def _params(*sem):
    return pltpu.CompilerParams(dimension_semantics=sem, vmem_limit_bytes=VMEM_LIMIT)


def _silu(x):
    return x * (1.0 / (1.0 + jnp.exp(-x)))


def _rms_rows(x, gain):
    ms = jnp.mean(x * x, axis=-1, keepdims=True)
    return x * lax.rsqrt(ms + RMS_EPS) * gain


def _rmsnorm_body(x_ref, g_ref, o_ref):
    o_ref[...] = _rms_rows(x_ref[...], g_ref[...]).astype(o_ref.dtype)


def rmsnorm(x, gain, out_dtype, tm):
    m, d = x.shape
    return pl.pallas_call(
        _rmsnorm_body,
        grid=(m // tm,),
        in_specs=[pl.BlockSpec((tm, d), lambda i: (i, 0)),
                  pl.BlockSpec((1, d), lambda i: (0, 0))],
        out_specs=pl.BlockSpec((tm, d), lambda i: (i, 0)),
        out_shape=jax.ShapeDtypeStruct((m, d), out_dtype),
        compiler_params=_params("arbitrary"),
        name="rmsnorm",
    )(x, gain.reshape(1, d))


def rmsnorm_untile(x3, gain, row0, rows, tm):
    tiles, _, d = x3.shape
    nb, b0 = rows // tm, row0 // tm
    return pl.pallas_call(
        _rmsnorm_body,
        grid=(tiles, nb),
        in_specs=[pl.BlockSpec((None, tm, d), lambda i, s: (i, b0 + s, 0)),
                  pl.BlockSpec((1, d), lambda i, s: (0, 0))],
        out_specs=pl.BlockSpec((tm, d), lambda i, s: (i * nb + s, 0)),
        out_shape=jax.ShapeDtypeStruct((tiles * rows, d), F32),
        compiler_params=_params("arbitrary", "arbitrary"),
        name="rmsnorm_untile",
    )(x3, gain.reshape(1, d))


FFN_ROW_CHUNK = 80


def _ffn_body(x_hbm, g_ref, wg_ref, wu_ref, wd_ref, o_ref, xn_ref, sem, *, tm):
    i = pl.program_id(0)
    j = pl.program_id(1)
    rc = FFN_ROW_CHUNK

    @pl.when(j == 0)
    def _():
        copy = pltpu.make_async_copy(x_hbm.at[pl.ds(pl.multiple_of(i * tm, 8), tm), :], o_ref, sem)
        copy.start()
        copy.wait()

        def body(r, c):
            rows = pl.ds(pl.multiple_of(r * rc, rc), rc)
            xn_ref[rows, :] = _rms_rows(o_ref[rows, :], g_ref[...]).astype(BF16)
            return c
        lax.fori_loop(0, tm // rc, body, 0)

    xn = xn_ref[...]
    g = _dot(xn, wg_ref[...].astype(BF16))
    u = _dot(xn, wu_ref[...].astype(BF16))
    a = (_silu(g) * u * 0.5).astype(BF16)
    o_ref[...] += _dot(a, wd_ref[...].astype(BF16))


def half_ffn(x, gain, w_gu, w_down, layer, tm):
    m, d = x.shape
    nj = D_FF // FFN_TF
    return pl.pallas_call(
        functools.partial(_ffn_body, tm=tm),
        grid=(m // tm, nj),
        in_specs=[pl.BlockSpec(memory_space=pl.ANY),
                  pl.BlockSpec((1, d), lambda i, j: (0, 0)),
                  pl.BlockSpec((None, d, FFN_TF), lambda i, j: (layer, 0, j)),
                  pl.BlockSpec((None, d, FFN_TF), lambda i, j: (layer, 0, j + nj)),
                  pl.BlockSpec((None, FFN_TF, d), lambda i, j: (layer, j, 0))],
        out_specs=pl.BlockSpec((tm, d), lambda i, j: (i, 0), pipeline_mode=pl.Buffered(1)),
        out_shape=jax.ShapeDtypeStruct((m, d), F32),
        scratch_shapes=[pltpu.VMEM((tm, d), BF16), pltpu.SemaphoreType.DMA(())],
        compiler_params=_params("arbitrary", "arbitrary"),
        name="half_ffn",
    )(x, gain.reshape(1, d), w_gu, w_gu, w_down)


PROMPT_TM = 1024
SAMPLE_ROWS = 16
TILE_ROWS = PROMPT_TM + SAMPLE_ROWS
MM_TN = 512


def _proj_body(a_ref, w_ref, *refs, has_scale, n_out):
    scale_ref = refs[0] if has_scale else None
    out_refs = refs[1:] if has_scale else refs
    acc = _dot(a_ref[...], w_ref[...].astype(BF16))
    for t in range(n_out):
        op_ref, os_ref = out_refs[2 * t], out_refs[2 * t + 1]
        val = acc * scale_ref[...] if (has_scale and op_ref.dtype == BF16) else acc
        op_ref[...] = val[:PROMPT_TM].astype(op_ref.dtype)
        os_ref[...] = val[PROMPT_TM:].astype(os_ref.dtype)


def proj_matmul(a, w, layer, col0, n, out_dtypes, scale=None):
    tiles = a.shape[0] // TILE_ROWS
    k_total = w.shape[1]
    tn = min(MM_TN, n)
    joff = col0 // tn
    in_specs = [pl.BlockSpec((TILE_ROWS, k_total), lambda i, j: (i, 0)),
                pl.BlockSpec((None, k_total, tn), lambda i, j: (layer, 0, j + joff))]
    args = [a, w]
    if scale is not None:
        in_specs.append(pl.BlockSpec((1, tn), lambda i, j: (0, j)))
        args.append(scale)
    out_specs, out_shape = [], []
    for dt in out_dtypes:
        out_specs += [pl.BlockSpec((PROMPT_TM, tn), lambda i, j: (i, j)),
                      pl.BlockSpec((None, SAMPLE_ROWS, tn), lambda i, j: (i, 0, j))]
        out_shape += [jax.ShapeDtypeStruct((tiles * PROMPT_TM, n), dt),
                      jax.ShapeDtypeStruct((tiles, SAMPLE_ROWS, n), dt)]
    outs = pl.pallas_call(
        functools.partial(_proj_body, has_scale=scale is not None, n_out=len(out_dtypes)),
        grid=(tiles, n // tn),
        in_specs=in_specs,
        out_specs=out_specs,
        out_shape=out_shape,
        compiler_params=_params("arbitrary", "arbitrary"),
        name="proj_matmul",
    )(*args)
    return [(outs[2 * t], outs[2 * t + 1][0]) for t in range(len(out_dtypes))]


def _out_proj_body(*refs, n_a):
    ap_refs, as_refs = refs[:n_a], refs[n_a:2 * n_a]
    w_ref, res_ref, o_ref, a_scr = refs[2 * n_a:]

    @pl.when(pl.program_id(1) == 0)
    def _():
        k0 = 0
        for ap_ref, as_ref in zip(ap_refs, as_refs):
            kk = ap_ref.shape[1]
            a_scr[0:PROMPT_TM, k0:k0 + kk] = ap_ref[...]
            a_scr[PROMPT_TM:TILE_ROWS, k0:k0 + kk] = as_ref[...]
            k0 += kk

    o_ref[...] = res_ref[...] + _dot(a_scr[...], w_ref[...].astype(BF16))


def out_proj_matmul(ap_list, as_list, w, layer, res):
    m, n = res.shape
    k_total = w.shape[1]
    tn = MM_TN
    in_specs = ([pl.BlockSpec((PROMPT_TM, a.shape[1]), lambda i, j: (i, 0)) for a in ap_list]
                + [pl.BlockSpec((SAMPLE_ROWS, a.shape[1]), lambda i, j: (0, 0)) for a in as_list]
                + [pl.BlockSpec((None, k_total, tn), lambda i, j: (layer, 0, j)),
                   pl.BlockSpec((TILE_ROWS, tn), lambda i, j: (i, j))])
    return pl.pallas_call(
        functools.partial(_out_proj_body, n_a=len(ap_list)),
        grid=(m // TILE_ROWS, n // tn),
        in_specs=in_specs,
        out_specs=pl.BlockSpec((TILE_ROWS, tn), lambda i, j: (i, j)),
        out_shape=jax.ShapeDtypeStruct((m, n), F32),
        scratch_shapes=[pltpu.VMEM((TILE_ROWS, k_total), BF16)],
        compiler_params=_params("arbitrary", "arbitrary"),
        name="out_proj_matmul",
    )(*ap_list, *as_list, w, res)


def _log_decay(alo, walpha, balpha):
    z = _dot(alo.astype(BF16), walpha) + balpha
    return -(jnp.maximum(-z, 0.0) + jnp.log1p(jnp.exp(-jnp.abs(z)))) / GLA_TAU


def _cumsum_rows(x):
    n = x.shape[0]
    rows = lax.broadcasted_iota(jnp.int32, x.shape, 0)
    s = 1
    while s < n:
        x = x + jnp.where(rows >= s, pltpu.roll(x, s, 0), 0.0)
        s *= 2
    return x


def _gla_chunk_head(q, k, v, r, cum, gain, st):
    C, SUB = GLA_CHUNK, GLA_SUB
    vb = v.astype(BF16)
    last = cum[C - 1:C, :]

    inter = _dot_nt((q * jnp.exp(cum)).astype(BF16), st.astype(BF16))

    rows_c = lax.broadcasted_iota(jnp.int32, (C, GLA_DK), 0)
    lane_c = lax.broadcasted_iota(jnp.int32, (SUB, C), 1)
    sub_j = lax.broadcasted_iota(jnp.int32, (SUB, GLA_DK), 0)
    diag_blocks = []
    att_t = jnp.zeros((C, C), F32)
    for blk in range(C // SUB):
        lo = blk * SUB
        q_b, k_b, c_b = q[lo:lo + SUB], k[lo:lo + SUB], cum[lo:lo + SUB]
        dg = jnp.zeros((SUB, C), F32)
        for i in range(SUB):
            rel = c_b[i:i + 1, :] - c_b
            dec = jnp.exp(jnp.where(sub_j <= i, rel, -jnp.inf))
            col = jnp.sum(q_b[i:i + 1, :] * k_b * dec, axis=-1, keepdims=True)
            dg = jnp.where(lane_c == lo + i, col, dg)
        diag_blocks.append(dg)
        if blk > 0:
            edge = cum[lo - 1:lo, :]
            in_blk = (rows_c >= lo) & (rows_c < lo + SUB)
            q_t = jnp.where(in_blk, q * jnp.exp(jnp.where(in_blk, cum - edge, 0.0)), 0.0)
            k_t = k[:lo] * jnp.exp(edge - cum[:lo])
            off = _dot_nt(k_t.astype(BF16), q_t.astype(BF16))
            att_t = att_t + jnp.concatenate([off, jnp.zeros((C - lo, C), F32)], axis=0)
    att_t = att_t + jnp.concatenate(diag_blocks, axis=0)

    o = inter + _dot_tn(att_t.astype(BF16), vb)
    y = _rms_rows(o, gain) * _silu(r)

    k_l = (k * jnp.exp(last - cum)).astype(BF16)
    st_new = st * jnp.exp(last) + _dot_tn(vb, k_l)
    return y, st_new


def _gla_chunk_body(q_ref, k_ref, v_ref, r_ref, alo_ref, wa_ref, ba_ref, gain_ref, o_ref, s_ref, st_ref):
    c = pl.program_id(1)
    nc = pl.num_programs(1)

    @pl.when(c == 0)
    def _():
        st_ref[...] = jnp.zeros_like(st_ref)

    cum = _cumsum_rows(_log_decay(alo_ref[...], wa_ref[...], ba_ref[...]))
    for h in range(GLA_HEADS):
        dk = slice(h * GLA_DK, (h + 1) * GLA_DK)
        dv = slice(h * GLA_DV, (h + 1) * GLA_DV)
        y, st_new = _gla_chunk_head(q_ref[:, dk] * GLA_DK ** -0.5, k_ref[:, dk], v_ref[:, dv], r_ref[:, dv],
                                    cum[:, dk], gain_ref[...], st_ref[h])
        o_ref[:, dv] = y.astype(o_ref.dtype)
        st_ref[h] = st_new

    @pl.when(c == nc - 1)
    def _():
        for h in range(GLA_HEADS):
            s_ref[h] = st_ref[h].T


def gla_prompt(proj, alo, w_alpha, b_alpha, out_gain, e, batch, seq):
    m = proj.shape[0]
    nc = seq // GLA_CHUNK
    C = GLA_CHUNK
    row = lambda b, c: b * nc + c
    return pl.pallas_call(
        _gla_chunk_body,
        grid=(batch, nc),
        in_specs=[pl.BlockSpec((C, GLA_QK), lambda b, c: (row(b, c), 0)),
                  pl.BlockSpec((C, GLA_QK), lambda b, c: (row(b, c), 1)),
                  pl.BlockSpec((C, GLA_V), lambda b, c: (row(b, c), 1)),
                  pl.BlockSpec((C, GLA_V), lambda b, c: (row(b, c), 2)),
                  pl.BlockSpec((C, LANES), lambda b, c: (row(b, c), 0)),
                  pl.BlockSpec((None, LANES, GLA_QK), lambda b, c: (e, 0, 0)),
                  pl.BlockSpec((None, 1, GLA_QK), lambda b, c: (e, 0, 0)),
                  pl.BlockSpec((None, 1, GLA_DV), lambda b, c: (e, 0, 0))],
        out_specs=[pl.BlockSpec((C, GLA_V), lambda b, c: (row(b, c), 0)),
                   pl.BlockSpec((None, GLA_HEADS, GLA_DK, GLA_DV), lambda b, c: (b, 0, 0, 0))],
        out_shape=[jax.ShapeDtypeStruct((m, GLA_V), BF16),
                   jax.ShapeDtypeStruct((batch, GLA_HEADS, GLA_DK, GLA_DV), F32)],
        scratch_shapes=[pltpu.VMEM((GLA_HEADS, GLA_DV, GLA_DK), F32)],
        compiler_params=_params("arbitrary", "arbitrary"),
        name="gla_prompt",
    )(proj, proj, proj, proj, alo, w_alpha, b_alpha, out_gain)


def _row_to_col(row, n):
    r = lax.broadcasted_iota(jnp.int32, (n, n), 0)
    c = lax.broadcasted_iota(jnp.int32, (n, n), 1)
    return jnp.sum(jnp.where(r == c, row, 0.0), axis=1, keepdims=True)


def _gla_step_body(q_ref, k_ref, v_ref, r_ref, alo_ref, wa_ref, ba_ref, gain_ref, s0_ref, o_ref, s_ref):
    b = pl.program_id(0)

    def pick(ref):
        blk = ref[...]
        rows = lax.broadcasted_iota(jnp.int32, blk.shape, 0)
        return jnp.sum(jnp.where(rows == b, blk, 0.0), axis=0, keepdims=True)

    q = pick(q_ref) * GLA_DK ** -0.5
    k = pick(k_ref)
    v = pick(v_ref)
    r = pick(r_ref)
    g = _log_decay(pick(alo_ref), wa_ref[...], ba_ref[...])
    a_col = _row_to_col(jnp.exp(g), GLA_DK)
    k_col = _row_to_col(k, GLA_DK)
    q_col = _row_to_col(q, GLA_DK)
    s_new = a_col * s0_ref[...] + k_col * v
    s_ref[...] = s_new
    o = jnp.sum(q_col * s_new, axis=0, keepdims=True)
    o_ref[...] = _rms_rows(o, gain_ref[...]) * _silu(r)


def gla_sample(proj, alo, w_alpha, b_alpha, out_gain, state, e, batch):
    rows = proj.shape[0]
    return pl.pallas_call(
        _gla_step_body,
        grid=(batch, GLA_HEADS),
        in_specs=[pl.BlockSpec((rows, GLA_DK), lambda b, h: (0, h)),
                  pl.BlockSpec((rows, GLA_DK), lambda b, h: (0, GLA_HEADS + h)),
                  pl.BlockSpec((rows, GLA_DV), lambda b, h: (0, GLA_HEADS + h)),
                  pl.BlockSpec((rows, GLA_DV), lambda b, h: (0, 2 * GLA_HEADS + h)),
                  pl.BlockSpec((rows, LANES), lambda b, h: (0, 0)),
                  pl.BlockSpec((None, LANES, GLA_DK), lambda b, h: (e, 0, h)),
                  pl.BlockSpec((None, 1, GLA_DK), lambda b, h: (e, 0, h)),
                  pl.BlockSpec((None, 1, GLA_DV), lambda b, h: (e, 0, 0)),
                  pl.BlockSpec((None, None, None, GLA_DK, GLA_DV), lambda b, h: (e, b, h, 0, 0))],
        out_specs=[pl.BlockSpec((None, 1, GLA_DV), lambda b, h: (b, 0, h)),
                   pl.BlockSpec((None, None, GLA_DK, GLA_DV), lambda b, h: (b, h, 0, 0))],
        out_shape=[jax.ShapeDtypeStruct((batch, 1, GLA_V), F32),
                   jax.ShapeDtypeStruct((batch, GLA_HEADS, GLA_DK, GLA_DV), F32)],
        compiler_params=_params("arbitrary", "arbitrary"),
        name="gla_sample",
    )(proj, proj, proj, proj, alo, w_alpha, b_alpha, out_gain, state)


CONV_TT = 256
CONV_HALO = 8


def _conv_prompt_body(gb_ref, gc_ref, hin_ref, pc_ref, ph_ref, w_ref, y_ref, buf_ref, u_ref):
    i = pl.program_id(1)
    nt = pl.num_programs(1)
    T, H = CONV_TT, CONV_HALO
    u = gc_ref[...] * hin_ref[...]
    prev = pc_ref[...] * ph_ref[...]
    u_ref[0:H, :] = jnp.where(i > 0, prev, 0.0)
    u_ref[H:H + T, :] = u
    w = w_ref[...]
    conv = w[0:1, :] * u_ref[H - 2:H - 2 + T, :] + w[1:2, :] * u_ref[H - 1:H - 1 + T, :] + w[2:3, :] * u
    y_ref[...] = (gb_ref[...] * conv).astype(y_ref.dtype)

    @pl.when(i == nt - 1)
    def _():
        buf_ref[...] = u_ref[H + T - (CONV_W - 1):H + T, :]


def conv_prompt(proj, conv_w, e, batch, seq):
    m = proj.shape[0]
    T, H = CONV_TT, CONV_HALO
    nt = seq // T
    row = lambda b, i: b * nt + i
    prev = lambda b, i: jnp.maximum((b * seq + i * T) // H - 1, 0)
    return pl.pallas_call(
        _conv_prompt_body,
        grid=(batch, nt),
        in_specs=[pl.BlockSpec((T, CONV_CH), lambda b, i: (row(b, i), 3)),
                  pl.BlockSpec((T, CONV_CH), lambda b, i: (row(b, i), 4)),
                  pl.BlockSpec((T, CONV_CH), lambda b, i: (row(b, i), 5)),
                  pl.BlockSpec((H, CONV_CH), lambda b, i: (prev(b, i), 4)),
                  pl.BlockSpec((H, CONV_CH), lambda b, i: (prev(b, i), 5)),
                  pl.BlockSpec((None, CONV_W, CONV_CH), lambda b, i: (e, 0, 0))],
        out_specs=[pl.BlockSpec((T, CONV_CH), lambda b, i: (row(b, i), 0)),
                   pl.BlockSpec((None, CONV_W - 1, CONV_CH), lambda b, i: (b, 0, 0))],
        out_shape=[jax.ShapeDtypeStruct((m, CONV_CH), BF16),
                   jax.ShapeDtypeStruct((batch, CONV_W - 1, CONV_CH), F32)],
        scratch_shapes=[pltpu.VMEM((H + T, CONV_CH), F32)],
        compiler_params=_params("arbitrary", "arbitrary"),
        name="conv_prompt",
    )(proj, proj, proj, proj, proj, conv_w)


def _conv_sample_body(gb_ref, gc_ref, hin_ref, buf_ref, w_ref, y_ref, nbuf_ref):
    u = gc_ref[...] * hin_ref[...]
    w = w_ref[...]
    b0 = buf_ref[0]
    b1 = buf_ref[1]
    y_ref[...] = gb_ref[...] * (w[0:1, :] * b0 + w[1:2, :] * b1 + w[2:3, :] * u)
    nbuf_ref[0] = b1
    nbuf_ref[1] = u


def conv_sample(proj, buf_t, conv_w, e):
    rows = proj.shape[0]
    return pl.pallas_call(
        _conv_sample_body,
        grid=(1,),
        in_specs=[pl.BlockSpec((rows, CONV_CH), lambda i: (0, 3)),
                  pl.BlockSpec((rows, CONV_CH), lambda i: (0, 4)),
                  pl.BlockSpec((rows, CONV_CH), lambda i: (0, 5)),
                  pl.BlockSpec((CONV_W - 1, rows, CONV_CH), lambda i: (0, 0, 0)),
                  pl.BlockSpec((None, CONV_W, CONV_CH), lambda i: (e, 0, 0))],
        out_specs=[pl.BlockSpec((rows, CONV_CH), lambda i: (0, 0)),
                   pl.BlockSpec((CONV_W - 1, rows, CONV_CH), lambda i: (0, 0, 0))],
        out_shape=[jax.ShapeDtypeStruct((rows, CONV_CH), F32),
                   jax.ShapeDtypeStruct((CONV_W - 1, rows, CONV_CH), F32)],
        compiler_params=_params("arbitrary"),
        name="conv_sample",
    )(proj, proj, proj, buf_t, conv_w)


def _bias_of_distance(n, bias_at):
    t = bias_at(_BUCKET0)
    for start, bucket in _BUCKET_STEPS:
        t = jnp.where(n >= start, bias_at(bucket), t)
    return t


def _lambda_full(lq1_ref, lk1_ref, lq2_ref, lk2_ref, lam_init):
    s1 = jnp.sum(lq1_ref[...] * lk1_ref[...], axis=-1, keepdims=True)
    s2 = jnp.sum(lq2_ref[...] * lk2_ref[...], axis=-1, keepdims=True)
    return jnp.exp(s1) - jnp.exp(s2) + lam_init


ATT_T = 256


def _attn_prompt_body(bias_ref, q_ref, k_ref, v_ref, lq1_ref, lk1_ref, lq2_ref, lk2_ref, sub_ref, o_ref,
                      tb_ref, *, lam_init, nq):
    h = pl.program_id(0)
    b = pl.program_id(1)
    qi = pl.program_id(2)
    T = ATT_T

    @pl.when((b == 0) & (qi == 0))
    def _():
        n0 = lax.broadcasted_iota(jnp.int32, (T, T), 0) - lax.broadcasted_iota(jnp.int32, (T, T), 1)
        bias_at = lambda bucket: bias_ref[bucket, h]
        tb_ref[0] = jnp.where(n0 >= 0, _bias_of_distance(n0, bias_at), NEG_INF)
        tb_ref[1] = _bias_of_distance(n0 + T, bias_at)

    q = q_ref[...]
    far_bias = bias_ref[NUM_BUCKETS - 1, h]
    lam = _lambda_full(lq1_ref, lk1_ref, lq2_ref, lk2_ref, lam_init)

    def one_map(c, n):
        cols = slice(c * ATTN_HD, (c + 1) * ATTN_HD)
        s = _dot_nt(q[:, cols], k_ref[0:(n + 1) * T, cols])
        far = [s[:, t * T:(t + 1) * T] for t in range(n - 1)]
        near = [s[:, t * T:(t + 1) * T] + tb_ref[n - t] for t in range(max(n - 1, 0), n + 1)]
        mx = near[0]
        for x in near[1:]:
            mx = jnp.maximum(mx, x)
        m = jnp.max(mx, axis=-1, keepdims=True)
        if far:
            fx = far[0]
            for x in far[1:]:
                fx = jnp.maximum(fx, x)
            m = jnp.maximum(m, jnp.max(fx, axis=-1, keepdims=True) + far_bias)
        m_far = m - far_bias
        ps = [jnp.exp(x - m_far) for x in far] + [jnp.exp(x - m) for x in near]
        tot = ps[0]
        for x in ps[1:]:
            tot = tot + x
        l = jnp.sum(tot, axis=-1, keepdims=True)
        p_all = jnp.concatenate([x.astype(BF16) for x in ps], axis=1)
        return _dot(p_all, v_ref[0:(n + 1) * T, :]) * (1.0 / l)

    for n in range(nq):
        @pl.when(qi == n)
        def _():
            o = one_map(0, n) - lam * one_map(1, n)
            o_ref[...] = (_rms_rows(o, sub_ref[...]) * (1.0 - lam_init)).astype(o_ref.dtype)


def attn_prompt(qb, kb, vb, rel_bias, lams, subln, a, lam_init, batch, seq):
    m = qb.shape[0]
    T = ATT_T
    nq = seq // T
    lam_spec = pl.BlockSpec((None, 1, ATTN_HD), lambda h, b, i: (a, 0, 0))
    return pl.pallas_call(
        functools.partial(_attn_prompt_body, lam_init=lam_init, nq=nq),
        grid=(ATTN_HEADS, batch, nq),
        in_specs=[pl.BlockSpec(memory_space=pltpu.SMEM),
                  pl.BlockSpec((T, ATTN_VD), lambda h, b, i: (b * nq + i, h)),
                  pl.BlockSpec((seq, ATTN_VD), lambda h, b, i: (b, h)),
                  pl.BlockSpec((seq, ATTN_VD), lambda h, b, i: (b, h)),
                  lam_spec, lam_spec, lam_spec, lam_spec,
                  pl.BlockSpec((None, 1, ATTN_VD), lambda h, b, i: (a, 0, 0))],
        out_specs=pl.BlockSpec((T, ATTN_VD), lambda h, b, i: (b * nq + i, h)),
        out_shape=jax.ShapeDtypeStruct((m, ATTN_HEADS * ATTN_VD), BF16),
        scratch_shapes=[pltpu.VMEM((2, T, T), F32)],
        compiler_params=_params("arbitrary", "arbitrary", "arbitrary"),
        name="attn_prompt",
    )(rel_bias, qb, kb, vb, *lams, subln)


SAMPLE_PAGES_PER_STEP = 4


def _attn_sample_body(pt_ref, qm_ref, *refs, lam_init, n_groups):
    G = SAMPLE_PAGES_PER_STEP
    k_refs, v_refs = refs[:G], refs[G:2 * G]
    (kn_ref, vn_ref, bias_ref, lq1_ref, lk1_ref, lq2_ref, lk2_ref, sub_ref, o_ref, m_ref, l_ref, acc_ref) = refs[2 * G:]
    g = pl.program_id(1)
    H = ATTN_HEADS
    R = 2 * H
    P = PAGE_SIZE
    qm = qm_ref[...]

    @pl.when(g == 0)
    def _():
        m_ref[...] = jnp.full(m_ref.shape, -jnp.inf, F32)
        l_ref[...] = jnp.zeros_like(l_ref)
        acc_ref[...] = jnp.zeros_like(acc_ref)

    def update(k_list, v_list, bias_list):
        n = k_list[0].shape[0]
        own = (lax.broadcasted_iota(jnp.int32, (R, n), 0) % H) == (lax.broadcasted_iota(jnp.int32, (R, n), 1) % H)
        s_list = [jnp.where(own, _dot_nt(qm, k2d) + bias, NEG_INF) for k2d, bias in zip(k_list, bias_list)]
        m_old = m_ref[...]
        m_new = m_old
        for s in s_list:
            m_new = jnp.maximum(m_new, jnp.max(s, axis=-1, keepdims=True))
        alpha = jnp.exp(m_old - m_new)
        l_new = alpha * l_ref[...]
        acc_new = alpha * acc_ref[...]
        for s, v2d in zip(s_list, v_list):
            p = jnp.exp(s - m_new)
            l_new = l_new + jnp.sum(p, axis=-1, keepdims=True)
            acc_new = acc_new + _dot(p.astype(BF16), v2d)
        m_ref[...] = m_new
        l_ref[...] = l_new
        acc_ref[...] = acc_new

    def pages(bias_list):
        k_list = [r[...].reshape(P * H, 2 * ATTN_HD).astype(BF16) for r in k_refs]
        v_list = [r[...].reshape(P * H, ATTN_VD).astype(BF16) for r in v_refs]
        update(k_list, v_list, bias_list)

    far_bias = bias_ref[:, NUM_BUCKETS - 1:NUM_BUCKETS]

    @pl.when(g < n_groups - 1)
    def _():
        pages([far_bias] * G)

    @pl.when(g == n_groups - 1)
    def _():
        dist = P - lax.broadcasted_iota(jnp.int32, (R, P * H), 1) // H
        near = _bias_of_distance(dist, lambda bucket: bias_ref[:, bucket:bucket + 1])
        pages([far_bias] * (G - 1) + [near])

    @pl.when(g == n_groups)
    def _():
        update([kn_ref[...].astype(BF16)], [vn_ref[...].astype(BF16)], [bias_ref[:, 0:1]])
        lam = _lambda_full(lq1_ref, lk1_ref, lq2_ref, lk2_ref, lam_init)
        on = acc_ref[...] * (1.0 / l_ref[...])
        o = on[:H] - lam * on[H:]
        o_ref[...] = _rms_rows(o, sub_ref[...]) * (1.0 - lam_init)


def attn_sample(qm, cache_k, cache_v, k_new, v_new, page_table, bias_rows, lams, subln, a, lam_init):
    batch, n_pages = page_table.shape
    H, P, G = ATTN_HEADS, PAGE_SIZE, SAMPLE_PAGES_PER_STEP
    R = 2 * H
    n_groups = n_pages // G

    def page_spec(t, width):
        return pl.BlockSpec((None, None, P, H, width),
                            lambda b, g, pt: (a, pt[b, jnp.minimum(g * G + t, n_pages - 1)], 0, 0, 0))

    lam_spec = pl.BlockSpec((None, 1, ATTN_HD), lambda b, g, pt: (a, 0, 0))
    grid_spec = pltpu.PrefetchScalarGridSpec(
        num_scalar_prefetch=1,
        grid=(batch, n_groups + 1),
        in_specs=([pl.BlockSpec((None, R, 2 * ATTN_HD), lambda b, g, pt: (b, 0, 0))]
                  + [page_spec(t, 2 * ATTN_HD) for t in range(G)]
                  + [page_spec(t, ATTN_VD) for t in range(G)]
                  + [pl.BlockSpec((None, H, 2 * ATTN_HD), lambda b, g, pt: (b, 0, 0)),
                     pl.BlockSpec((None, H, ATTN_VD), lambda b, g, pt: (b, 0, 0)),
                     pl.BlockSpec((R, NUM_BUCKETS), lambda b, g, pt: (0, 0)),
                     lam_spec, lam_spec, lam_spec, lam_spec,
                     pl.BlockSpec((None, 1, ATTN_VD), lambda b, g, pt: (a, 0, 0))]),
        out_specs=pl.BlockSpec((None, H, ATTN_VD), lambda b, g, pt: (b, 0, 0)),
        scratch_shapes=[pltpu.VMEM((R, 1), F32), pltpu.VMEM((R, 1), F32), pltpu.VMEM((R, ATTN_VD), F32)],
    )
    return pl.pallas_call(
        functools.partial(_attn_sample_body, lam_init=lam_init, n_groups=n_groups),
        grid_spec=grid_spec,
        out_shape=jax.ShapeDtypeStruct((batch, H, ATTN_VD), F32),
        compiler_params=_params("arbitrary", "arbitrary"),
        name="attn_sample",
    )(page_table, qm, *([cache_k] * G), *([cache_v] * G), k_new, v_new, bias_rows, *lams, subln)


NORM_TM = 208


def _pad_rows(x, rows):
    return jnp.pad(x, ((0, rows - x.shape[0]), (0, 0)))


def kernel(x_prompt, x_sample, cache_k, cache_v, state_gla, state_conv, page_table, rel_bias, norm_ffn1, w_ffn1_gu, w_ffn1_down, norm_mix, norm_ffn2, w_ffn2_gu, w_ffn2_down, norm_final, even_w_in, gla_w_alpha, gla_b_alpha, gla_out_norm, short_conv_w, even_w_out, attn_w_qkv, attn_lambda_q1, attn_lambda_k1, attn_lambda_q2, attn_lambda_k2, attn_subln, attn_w_o):
    bp, tp, d = x_prompt.shape
    bs = x_sample.shape[0]
    depth = norm_ffn1.shape[0]
    n_even = even_w_in.shape[0]
    n_odd = attn_w_qkv.shape[0]

    w1_gu, w1_d, w2_gu, w2_d = w_ffn1_gu, w_ffn1_down, w_ffn2_gu, w_ffn2_down
    a0 = 2 * GLA_QK + 2 * GLA_V
    w_in_main = jnp.concatenate([even_w_in[:, :, :a0], even_w_in[:, :, a0 + GLA_RANK:]], axis=-1).astype(BF16)
    w_in_alo = jnp.pad(even_w_in[:, :, a0:a0 + GLA_RANK], ((0, 0), (0, 0), (0, LANES - GLA_RANK))).astype(BF16)
    w_alpha = jnp.pad(gla_w_alpha, ((0, 0), (0, LANES - GLA_RANK), (0, 0))).astype(BF16)
    b_alpha = gla_b_alpha.reshape(n_even, 1, GLA_QK)
    out_gain = gla_out_norm.reshape(n_even, 1, GLA_DV)
    w_even_out, w_qkv, w_o = even_w_out, attn_w_qkv, attn_w_o
    lams = tuple(v.reshape(n_odd, 1, ATTN_HD) for v in (attn_lambda_q1, attn_lambda_k1, attn_lambda_q2, attn_lambda_k2))
    subln = attn_subln.reshape(n_odd, 1, ATTN_VD)
    q_scale = jnp.full((1, ATTN_QK), ATTN_SCALE, F32)
    bias_rows = jnp.tile(rel_bias.T, (2, 1))
    n_in = w_in_main.shape[-1]

    tiles = (bp * tp) // PROMPT_TM
    xs_pad = _pad_rows(x_sample.reshape(bs, d), SAMPLE_ROWS)
    x = jnp.concatenate([x_prompt.reshape(tiles, PROMPT_TM, d),
                         jnp.broadcast_to(xs_pad[None], (tiles, SAMPLE_ROWS, d))], axis=1).reshape(tiles * TILE_ROWS, d)
    conv_t = jnp.pad(jnp.swapaxes(state_conv, 1, 2), ((0, 0), (0, 0), (0, SAMPLE_ROWS - bs), (0, 0)))

    k_p, v_p, gla_p, conv_p = [], [], [], []
    k_s, v_s, gla_s, conv_s = [], [], [], []

    for l in range(depth):
        x = half_ffn(x, norm_ffn1[l], w1_gu, w1_d, l, TILE_ROWS)
        hn = rmsnorm(x, norm_mix[l], BF16, NORM_TM)
        if l % 2 == 0:
            e = l // 2
            ((proj_p, proj_s),) = proj_matmul(hn, w_in_main, e, 0, n_in, (F32,))
            ((alo_p, alo_s),) = proj_matmul(hn, w_in_alo, e, 0, LANES, (F32,))
            og_p, st_p = gla_prompt(proj_p, alo_p, w_alpha, b_alpha, out_gain, e, bp, tp)
            oc_p, buf_p = conv_prompt(proj_p, short_conv_w, e, bp, tp)
            og_s, st_s = gla_sample(proj_s, alo_s, w_alpha, b_alpha, out_gain, state_gla, e, bs)
            oc_s, buf_s = conv_sample(proj_s, conv_t[e], short_conv_w, e)
            og_s = _pad_rows(og_s.reshape(bs, GLA_V), SAMPLE_ROWS).astype(BF16)
            x = out_proj_matmul([og_p, oc_p], [og_s, oc_s.astype(BF16)], w_even_out, e, x)
            gla_p.append(st_p)
            conv_p.append(buf_p)
            gla_s.append(st_s)
            conv_s.append(jnp.swapaxes(buf_s[:, :bs], 0, 1))
        else:
            a = l // 2
            lam_init = 0.8 - 0.6 * math.exp(-0.3 * l)
            ((qb_p, qb_s),) = proj_matmul(hn, w_qkv, a, 0, ATTN_QK, (BF16,), scale=q_scale)
            (kf_p, kf_s), (kb_p, _) = proj_matmul(hn, w_qkv, a, ATTN_QK, ATTN_QK, (F32, BF16))
            (vf_p, vf_s), (vb_p, _) = proj_matmul(hn, w_qkv, a, 2 * ATTN_QK, ATTN_QK, (F32, BF16))
            o_p = attn_prompt(qb_p, kb_p, vb_p, rel_bias, lams, subln, a, lam_init, bp, tp)
            q4 = qb_s[:bs].reshape(bs, ATTN_HEADS, 2, ATTN_HD)
            z4 = jnp.zeros_like(q4[:, :, 0])
            qm = jnp.concatenate([jnp.concatenate([q4[:, :, 0], z4], axis=-1),
                                  jnp.concatenate([z4, q4[:, :, 1]], axis=-1)], axis=1)
            kn = kf_s[:bs].reshape(bs, ATTN_HEADS, 2 * ATTN_HD)
            vn = vf_s[:bs].reshape(bs, ATTN_HEADS, ATTN_VD)
            o_s = attn_sample(qm, cache_k, cache_v, kn, vn, page_table, bias_rows, lams, subln, a, lam_init)
            o_s = _pad_rows(o_s.reshape(bs, ATTN_HEADS * ATTN_VD), SAMPLE_ROWS).astype(BF16)
            x = out_proj_matmul([o_p], [o_s], w_o, a, x)
            k_p.append(kf_p.reshape(bp, tp, ATTN_HEADS, 2 * ATTN_HD))
            v_p.append(vf_p.reshape(bp, tp, ATTN_HEADS, ATTN_VD))
            k_s.append(kn.reshape(bs, 1, ATTN_HEADS, 2 * ATTN_HD))
            v_s.append(vn.reshape(bs, 1, ATTN_HEADS, ATTN_VD))
        x = half_ffn(x, norm_ffn2[l], w2_gu, w2_d, l, TILE_ROWS)

    x3 = x.reshape(tiles, TILE_ROWS, d)
    y_p = rmsnorm_untile(x3, norm_final, 0, PROMPT_TM, 256).reshape(bp, tp, d)
    y_s = rmsnorm_untile(x3[:1], norm_final, PROMPT_TM, SAMPLE_ROWS, SAMPLE_ROWS)[:bs].reshape(bs, 1, d)
    return (y_p, y_s,
            jnp.stack(k_p), jnp.stack(v_p), jnp.stack(gla_p), jnp.stack(conv_p),
            jnp.stack(k_s), jnp.stack(v_s), jnp.stack(gla_s), jnp.stack(conv_s))
```

```python
import functools
import math

import numpy as np
import jax
import jax.numpy as jnp
from jax import lax
from jax.experimental import pallas as pl
from jax.experimental.pallas import tpu as pltpu

F32 = jnp.float32
BF16 = jnp.bfloat16

D_MODEL = 4096
D_FF = 11008
RMS_EPS = 1e-6
PAGE_SIZE = 128

GLA_HEADS = 4
GLA_DK = 256
GLA_DV = 512
GLA_RANK = 16
GLA_TAU = 16.0
GLA_CHUNK = 64
GLA_SUB = 16
GLA_QK = GLA_HEADS * GLA_DK
GLA_V = GLA_HEADS * GLA_DV
CONV_CH = 2048
CONV_W = 3

ATTN_HEADS = 16
ATTN_HD = 128
ATTN_VD = 256
ATTN_QK = ATTN_HEADS * 2 * ATTN_HD
ATTN_SCALE = ATTN_HD ** -0.5
NEG_INF = -1e30
NUM_BUCKETS = 32
MAX_DISTANCE = 128

LANES = 128
FFN_TF = 256
VMEM_LIMIT = 60 * 1024 * 1024


def _bucket_steps():
    max_exact = NUM_BUCKETS // 2
    table = []
    n = 0
    while True:
        if n < max_exact:
            b = n
        else:
            b = min(max_exact + int(math.log(n / max_exact) / math.log(MAX_DISTANCE / max_exact)
                                    * (NUM_BUCKETS - max_exact)), NUM_BUCKETS - 1)
        table.append(b)
        if b == NUM_BUCKETS - 1:
            break
        n += 1
    steps = [(i, table[i]) for i in range(1, len(table)) if table[i] != table[i - 1]]
    return table[0], steps


_BUCKET0, _BUCKET_STEPS = _bucket_steps()


def _dot(a, b):
    return jnp.dot(a, b, preferred_element_type=F32)


def _dot_nt(a, b):
    return lax.dot_general(a, b, (((1,), (1,)), ((), ())), preferred_element_type=F32)


def _dot_tn(a, b):
    return lax.dot_general(a, b, (((0,), (0,)), ((), ())), preferred_element_type=F32)


def _params(*sem):
    return pltpu.CompilerParams(dimension_semantics=sem, vmem_limit_bytes=VMEM_LIMIT)


def _silu(x):
    return x * (1.0 / (1.0 + jnp.exp(-x)))


def _rms_rows(x, gain):
    ms = jnp.mean(x * x, axis=-1, keepdims=True)
    return x * lax.rsqrt(ms + RMS_EPS) * gain


def _rmsnorm_body(x_ref, g_ref, o_ref):
    o_ref[...] = _rms_rows(x_ref[...], g_ref[...]).astype(o_ref.dtype)


def rmsnorm_untile(x3, gain, row0, rows, tm):
    tiles, _, d = x3.shape
    nb, b0 = rows // tm, row0 // tm
    return pl.pallas_call(
        _rmsnorm_body,
        grid=(tiles, nb),
        in_specs=[pl.BlockSpec((None, tm, d), lambda i, s: (i, b0 + s, 0)),
                  pl.BlockSpec((1, d), lambda i, s: (0, 0))],
        out_specs=pl.BlockSpec((tm, d), lambda i, s: (i * nb + s, 0)),
        out_shape=jax.ShapeDtypeStruct((tiles * rows, d), F32),
        compiler_params=_params("arbitrary", "arbitrary"),
        name="rmsnorm_untile",
    )(x3, gain.reshape(1, d))


FFN_ROW_CHUNK = 80


def _ffn_body(*refs, tm, nj, emit_norm):
    if emit_norm:
        x_hbm, g_ref, g2_ref, wg_ref, wu_ref, wd_ref, o_ref, xn_ref, sem = refs
    else:
        x_hbm, g_ref, wg_ref, wu_ref, wd_ref, o_ref, xn_ref, sem = refs
    i = pl.program_id(0)
    j = pl.program_id(1)
    rc = FFN_ROW_CHUNK

    @pl.when(j == 0)
    def _():
        copy = pltpu.make_async_copy(x_hbm.at[pl.ds(pl.multiple_of(i * tm, 8), tm), :], o_ref, sem)
        copy.start()
        copy.wait()

        def body(r, c):
            rows = pl.ds(pl.multiple_of(r * rc, rc), rc)
            xn_ref[rows, :] = _rms_rows(o_ref[rows, :], g_ref[...]).astype(BF16)
            return c
        lax.fori_loop(0, tm // rc, body, 0)

    xn = xn_ref[...]
    g = _dot(xn, wg_ref[...].astype(BF16))
    u = _dot(xn, wu_ref[...].astype(BF16))
    a = (_silu(g) * u * 0.5).astype(BF16)
    o_ref[...] += _dot(a, wd_ref[...].astype(BF16))

    if emit_norm:
        @pl.when(j == nj - 1)
        def _():
            def body(r, c):
                rows = pl.ds(pl.multiple_of(r * rc, rc), rc)
                xn_ref[rows, :] = _rms_rows(o_ref[rows, :], g2_ref[...]).astype(BF16)
                return c
            lax.fori_loop(0, tm // rc, body, 0)


def half_ffn(x, gain, w_gu, w_down, layer, tm, next_gain=None):
    m, d = x.shape
    nj = D_FF // FFN_TF
    emit = next_gain is not None
    tile = lambda: pl.BlockSpec((tm, d), lambda i, j: (i, 0), pipeline_mode=pl.Buffered(1))
    gain_spec = pl.BlockSpec((1, d), lambda i, j: (0, 0))
    in_specs = [pl.BlockSpec(memory_space=pl.ANY), gain_spec] + ([gain_spec] if emit else []) + [
        pl.BlockSpec((None, d, FFN_TF), lambda i, j: (layer, 0, j)),
        pl.BlockSpec((None, d, FFN_TF), lambda i, j: (layer, 0, j + nj)),
        pl.BlockSpec((None, FFN_TF, d), lambda i, j: (layer, j, 0))]
    gains = [gain.reshape(1, d)] + ([next_gain.reshape(1, d)] if emit else [])
    xn_buf = [] if emit else [pltpu.VMEM((tm, d), BF16)]
    return pl.pallas_call(
        functools.partial(_ffn_body, tm=tm, nj=nj, emit_norm=emit),
        grid=(m // tm, nj),
        in_specs=in_specs,
        out_specs=[tile(), tile()] if emit else tile(),
        out_shape=([jax.ShapeDtypeStruct((m, d), F32), jax.ShapeDtypeStruct((m, d), BF16)] if emit
                   else jax.ShapeDtypeStruct((m, d), F32)),
        scratch_shapes=xn_buf + [pltpu.SemaphoreType.DMA(())],
        compiler_params=_params("arbitrary", "arbitrary"),
        name="half_ffn",
    )(x, *gains, w_gu, w_gu, w_down)


PROMPT_TM = 1024
SAMPLE_ROWS = 16
TILE_ROWS = PROMPT_TM + SAMPLE_ROWS
MM_TN = 512


def _proj_body(a_ref, w_ref, *refs, has_scale, n_out):
    scale_ref = refs[0] if has_scale else None
    out_refs = refs[1:] if has_scale else refs
    acc = _dot(a_ref[...], w_ref[...].astype(BF16))
    for t in range(n_out):
        op_ref, os_ref = out_refs[2 * t], out_refs[2 * t + 1]
        val = acc * scale_ref[...] if (has_scale and op_ref.dtype == BF16) else acc
        op_ref[...] = val[:PROMPT_TM].astype(op_ref.dtype)
        os_ref[...] = val[PROMPT_TM:].astype(os_ref.dtype)


def proj_matmul(a, w, layer, col0, n, out_dtypes, scale=None):
    tiles = a.shape[0] // TILE_ROWS
    k_total = w.shape[1]
    tn = min(MM_TN, n)
    joff = col0 // tn
    in_specs = [pl.BlockSpec((TILE_ROWS, k_total), lambda i, j: (i, 0)),
                pl.BlockSpec((None, k_total, tn), lambda i, j: (layer, 0, j + joff))]
    args = [a, w]
    if scale is not None:
        in_specs.append(pl.BlockSpec((1, tn), lambda i, j: (0, j)))
        args.append(scale)
    out_specs, out_shape = [], []
    for dt in out_dtypes:
        out_specs += [pl.BlockSpec((PROMPT_TM, tn), lambda i, j: (i, j)),
                      pl.BlockSpec((None, SAMPLE_ROWS, tn), lambda i, j: (i, 0, j))]
        out_shape += [jax.ShapeDtypeStruct((tiles * PROMPT_TM, n), dt),
                      jax.ShapeDtypeStruct((tiles, SAMPLE_ROWS, n), dt)]
    outs = pl.pallas_call(
        functools.partial(_proj_body, has_scale=scale is not None, n_out=len(out_dtypes)),
        grid=(tiles, n // tn),
        in_specs=in_specs,
        out_specs=out_specs,
        out_shape=out_shape,
        compiler_params=_params("arbitrary", "arbitrary"),
        name="proj_matmul",
    )(*args)
    return [(outs[2 * t], outs[2 * t + 1][0]) for t in range(len(out_dtypes))]


def _out_proj_body(*refs, n_a):
    ap_refs, as_refs = refs[:n_a], refs[n_a:2 * n_a]
    w_ref, res_ref, o_ref, a_scr = refs[2 * n_a:]

    @pl.when(pl.program_id(1) == 0)
    def _():
        k0 = 0
        for ap_ref, as_ref in zip(ap_refs, as_refs):
            kk = ap_ref.shape[1]
            a_scr[0:PROMPT_TM, k0:k0 + kk] = ap_ref[...]
            a_scr[PROMPT_TM:TILE_ROWS, k0:k0 + kk] = as_ref[...]
            k0 += kk

    o_ref[...] = res_ref[...] + _dot(a_scr[...], w_ref[...].astype(BF16))


def out_proj_matmul(ap_list, as_list, w, layer, res):
    m, n = res.shape
    k_total = w.shape[1]
    tn = MM_TN
    in_specs = ([pl.BlockSpec((PROMPT_TM, a.shape[1]), lambda i, j: (i, 0)) for a in ap_list]
                + [pl.BlockSpec((SAMPLE_ROWS, a.shape[1]), lambda i, j: (0, 0)) for a in as_list]
                + [pl.BlockSpec((None, k_total, tn), lambda i, j: (layer, 0, j)),
                   pl.BlockSpec((TILE_ROWS, tn), lambda i, j: (i, j))])
    return pl.pallas_call(
        functools.partial(_out_proj_body, n_a=len(ap_list)),
        grid=(m // TILE_ROWS, n // tn),
        in_specs=in_specs,
        out_specs=pl.BlockSpec((TILE_ROWS, tn), lambda i, j: (i, j)),
        out_shape=jax.ShapeDtypeStruct((m, n), F32),
        scratch_shapes=[pltpu.VMEM((TILE_ROWS, k_total), BF16)],
        compiler_params=_params("arbitrary", "arbitrary"),
        name="out_proj_matmul",
    )(*ap_list, *as_list, w, res)


def _log_decay(alo, walpha, balpha):
    z = _dot(alo.astype(BF16), walpha) + balpha
    return -(jnp.maximum(-z, 0.0) + jnp.log1p(jnp.exp(-jnp.abs(z)))) / GLA_TAU


def _cumsum_rows(x):
    n = x.shape[0]
    rows = lax.broadcasted_iota(jnp.int32, x.shape, 0)
    s = 1
    while s < n:
        x = x + jnp.where(rows >= s, pltpu.roll(x, s, 0), 0.0)
        s *= 2
    return x


def _gla_chunk_head(q, k, v, r, cum, gain, st):
    C, SUB = GLA_CHUNK, GLA_SUB
    vb = v.astype(BF16)
    last = cum[C - 1:C, :]

    inter = _dot_nt((q * jnp.exp(cum)).astype(BF16), st.astype(BF16))

    rows_c = lax.broadcasted_iota(jnp.int32, (C, GLA_DK), 0)
    lane_c = lax.broadcasted_iota(jnp.int32, (SUB, C), 1)
    sub_j = lax.broadcasted_iota(jnp.int32, (SUB, GLA_DK), 0)
    diag_blocks = []
    att_t = jnp.zeros((C, C), F32)
    for blk in range(C // SUB):
        lo = blk * SUB
        q_b, k_b, c_b = q[lo:lo + SUB], k[lo:lo + SUB], cum[lo:lo + SUB]
        dg = jnp.zeros((SUB, C), F32)
        for i in range(SUB):
            rel = c_b[i:i + 1, :] - c_b
            dec = jnp.exp(jnp.where(sub_j <= i, rel, -jnp.inf))
            col = jnp.sum(q_b[i:i + 1, :] * k_b * dec, axis=-1, keepdims=True)
            dg = jnp.where(lane_c == lo + i, col, dg)
        diag_blocks.append(dg)
        if blk > 0:
            edge = cum[lo - 1:lo, :]
            in_blk = (rows_c >= lo) & (rows_c < lo + SUB)
            q_t = jnp.where(in_blk, q * jnp.exp(jnp.where(in_blk, cum - edge, 0.0)), 0.0)
            k_t = k[:lo] * jnp.exp(edge - cum[:lo])
            off = _dot_nt(k_t.astype(BF16), q_t.astype(BF16))
            att_t = att_t + jnp.concatenate([off, jnp.zeros((C - lo, C), F32)], axis=0)
    att_t = att_t + jnp.concatenate(diag_blocks, axis=0)

    o = inter + _dot_tn(att_t.astype(BF16), vb)
    y = _rms_rows(o, gain) * _silu(r)

    k_l = (k * jnp.exp(last - cum)).astype(BF16)
    st_new = st * jnp.exp(last) + _dot_tn(vb, k_l)
    return y, st_new


def _gla_chunk_body(q_ref, k_ref, v_ref, r_ref, alo_ref, wa_ref, ba_ref, gain_ref, o_ref, s_ref, st_ref):
    c = pl.program_id(1)
    nc = pl.num_programs(1)

    @pl.when(c == 0)
    def _():
        st_ref[...] = jnp.zeros_like(st_ref)

    cum = _cumsum_rows(_log_decay(alo_ref[...], wa_ref[...], ba_ref[...]))
    for h in range(GLA_HEADS):
        dk = slice(h * GLA_DK, (h + 1) * GLA_DK)
        dv = slice(h * GLA_DV, (h + 1) * GLA_DV)
        y, st_new = _gla_chunk_head(q_ref[:, dk] * GLA_DK ** -0.5, k_ref[:, dk], v_ref[:, dv], r_ref[:, dv],
                                    cum[:, dk], gain_ref[...], st_ref[h])
        o_ref[:, dv] = y.astype(o_ref.dtype)
        st_ref[h] = st_new

    @pl.when(c == nc - 1)
    def _():
        for h in range(GLA_HEADS):
            s_ref[h] = st_ref[h].T


def gla_prompt(proj, alo, w_alpha, b_alpha, out_gain, e, batch, seq):
    m = proj.shape[0]
    nc = seq // GLA_CHUNK
    C = GLA_CHUNK
    row = lambda b, c: b * nc + c
    return pl.pallas_call(
        _gla_chunk_body,
        grid=(batch, nc),
        in_specs=[pl.BlockSpec((C, GLA_QK), lambda b, c: (row(b, c), 0)),
                  pl.BlockSpec((C, GLA_QK), lambda b, c: (row(b, c), 1)),
                  pl.BlockSpec((C, GLA_V), lambda b, c: (row(b, c), 1)),
                  pl.BlockSpec((C, GLA_V), lambda b, c: (row(b, c), 2)),
                  pl.BlockSpec((C, LANES), lambda b, c: (row(b, c), 0)),
                  pl.BlockSpec((None, LANES, GLA_QK), lambda b, c: (e, 0, 0)),
                  pl.BlockSpec((None, 1, GLA_QK), lambda b, c: (e, 0, 0)),
                  pl.BlockSpec((None, 1, GLA_DV), lambda b, c: (e, 0, 0))],
        out_specs=[pl.BlockSpec((C, GLA_V), lambda b, c: (row(b, c), 0)),
                   pl.BlockSpec((None, GLA_HEADS, GLA_DK, GLA_DV), lambda b, c: (b, 0, 0, 0))],
        out_shape=[jax.ShapeDtypeStruct((m, GLA_V), BF16),
                   jax.ShapeDtypeStruct((batch, GLA_HEADS, GLA_DK, GLA_DV), F32)],
        scratch_shapes=[pltpu.VMEM((GLA_HEADS, GLA_DV, GLA_DK), F32)],
        compiler_params=_params("arbitrary", "arbitrary"),
        name="gla_prompt",
    )(proj, proj, proj, proj, alo, w_alpha, b_alpha, out_gain)


def _row_to_col(row, n):
    r = lax.broadcasted_iota(jnp.int32, (n, n), 0)
    c = lax.broadcasted_iota(jnp.int32, (n, n), 1)
    return jnp.sum(jnp.where(r == c, row, 0.0), axis=1, keepdims=True)


def _gla_step_body(q_ref, k_ref, v_ref, r_ref, alo_ref, wa_ref, ba_ref, gain_ref, s0_ref, o_ref, s_ref):
    b = pl.program_id(0)

    def pick(ref):
        blk = ref[...]
        rows = lax.broadcasted_iota(jnp.int32, blk.shape, 0)
        return jnp.sum(jnp.where(rows == b, blk, 0.0), axis=0, keepdims=True)

    q = pick(q_ref) * GLA_DK ** -0.5
    k = pick(k_ref)
    v = pick(v_ref)
    r = pick(r_ref)
    g = _log_decay(pick(alo_ref), wa_ref[...], ba_ref[...])
    a_col = _row_to_col(jnp.exp(g), GLA_DK)
    k_col = _row_to_col(k, GLA_DK)
    q_col = _row_to_col(q, GLA_DK)
    s_new = a_col * s0_ref[...] + k_col * v
    s_ref[...] = s_new
    o = jnp.sum(q_col * s_new, axis=0, keepdims=True)
    o_ref[...] = _rms_rows(o, gain_ref[...]) * _silu(r)


def gla_sample(proj, alo, w_alpha, b_alpha, out_gain, state, e, batch):
    rows = proj.shape[0]
    return pl.pallas_call(
        _gla_step_body,
        grid=(batch, GLA_HEADS),
        in_specs=[pl.BlockSpec((rows, GLA_DK), lambda b, h: (0, h)),
                  pl.BlockSpec((rows, GLA_DK), lambda b, h: (0, GLA_HEADS + h)),
                  pl.BlockSpec((rows, GLA_DV), lambda b, h: (0, GLA_HEADS + h)),
                  pl.BlockSpec((rows, GLA_DV), lambda b, h: (0, 2 * GLA_HEADS + h)),
                  pl.BlockSpec((rows, LANES), lambda b, h: (0, 0)),
                  pl.BlockSpec((None, LANES, GLA_DK), lambda b, h: (e, 0, h)),
                  pl.BlockSpec((None, 1, GLA_DK), lambda b, h: (e, 0, h)),
                  pl.BlockSpec((None, 1, GLA_DV), lambda b, h: (e, 0, 0)),
                  pl.BlockSpec((None, None, None, GLA_DK, GLA_DV), lambda b, h: (e, b, h, 0, 0))],
        out_specs=[pl.BlockSpec((None, 1, GLA_DV), lambda b, h: (b, 0, h)),
                   pl.BlockSpec((None, None, GLA_DK, GLA_DV), lambda b, h: (b, h, 0, 0))],
        out_shape=[jax.ShapeDtypeStruct((batch, 1, GLA_V), F32),
                   jax.ShapeDtypeStruct((batch, GLA_HEADS, GLA_DK, GLA_DV), F32)],
        compiler_params=_params("arbitrary", "arbitrary"),
        name="gla_sample",
    )(proj, proj, proj, proj, alo, w_alpha, b_alpha, out_gain, state)


CONV_TT = 256
CONV_HALO = 8


def _conv_prompt_body(gb_ref, gc_ref, hin_ref, pc_ref, ph_ref, w_ref, y_ref, buf_ref, u_ref):
    i = pl.program_id(1)
    nt = pl.num_programs(1)
    T, H = CONV_TT, CONV_HALO
    u = gc_ref[...] * hin_ref[...]
    prev = pc_ref[...] * ph_ref[...]
    u_ref[0:H, :] = jnp.where(i > 0, prev, 0.0)
    u_ref[H:H + T, :] = u
    w = w_ref[...]
    conv = w[0:1, :] * u_ref[H - 2:H - 2 + T, :] + w[1:2, :] * u_ref[H - 1:H - 1 + T, :] + w[2:3, :] * u
    y_ref[...] = (gb_ref[...] * conv).astype(y_ref.dtype)

    @pl.when(i == nt - 1)
    def _():
        buf_ref[...] = u_ref[H + T - (CONV_W - 1):H + T, :]


def conv_prompt(proj, conv_w, e, batch, seq):
    m = proj.shape[0]
    T, H = CONV_TT, CONV_HALO
    nt = seq // T
    row = lambda b, i: b * nt + i
    prev = lambda b, i: jnp.maximum((b * seq + i * T) // H - 1, 0)
    return pl.pallas_call(
        _conv_prompt_body,
        grid=(batch, nt),
        in_specs=[pl.BlockSpec((T, CONV_CH), lambda b, i: (row(b, i), 3)),
                  pl.BlockSpec((T, CONV_CH), lambda b, i: (row(b, i), 4)),
                  pl.BlockSpec((T, CONV_CH), lambda b, i: (row(b, i), 5)),
                  pl.BlockSpec((H, CONV_CH), lambda b, i: (prev(b, i), 4)),
                  pl.BlockSpec((H, CONV_CH), lambda b, i: (prev(b, i), 5)),
                  pl.BlockSpec((None, CONV_W, CONV_CH), lambda b, i: (e, 0, 0))],
        out_specs=[pl.BlockSpec((T, CONV_CH), lambda b, i: (row(b, i), 0)),
                   pl.BlockSpec((None, CONV_W - 1, CONV_CH), lambda b, i: (b, 0, 0))],
        out_shape=[jax.ShapeDtypeStruct((m, CONV_CH), BF16),
                   jax.ShapeDtypeStruct((batch, CONV_W - 1, CONV_CH), F32)],
        scratch_shapes=[pltpu.VMEM((H + T, CONV_CH), F32)],
        compiler_params=_params("arbitrary", "arbitrary"),
        name="conv_prompt",
    )(proj, proj, proj, proj, proj, conv_w)


def _conv_sample_body(gb_ref, gc_ref, hin_ref, buf_ref, w_ref, y_ref, nbuf_ref):
    u = gc_ref[...] * hin_ref[...]
    w = w_ref[...]
    b0 = buf_ref[0]
    b1 = buf_ref[1]
    y_ref[...] = gb_ref[...] * (w[0:1, :] * b0 + w[1:2, :] * b1 + w[2:3, :] * u)
    nbuf_ref[0] = b1
    nbuf_ref[1] = u


def conv_sample(proj, buf_t, conv_w, e):
    rows = proj.shape[0]
    return pl.pallas_call(
        _conv_sample_body,
        grid=(1,),
        in_specs=[pl.BlockSpec((rows, CONV_CH), lambda i: (0, 3)),
                  pl.BlockSpec((rows, CONV_CH), lambda i: (0, 4)),
                  pl.BlockSpec((rows, CONV_CH), lambda i: (0, 5)),
                  pl.BlockSpec((CONV_W - 1, rows, CONV_CH), lambda i: (0, 0, 0)),
                  pl.BlockSpec((None, CONV_W, CONV_CH), lambda i: (e, 0, 0))],
        out_specs=[pl.BlockSpec((rows, CONV_CH), lambda i: (0, 0)),
                   pl.BlockSpec((CONV_W - 1, rows, CONV_CH), lambda i: (0, 0, 0))],
        out_shape=[jax.ShapeDtypeStruct((rows, CONV_CH), F32),
                   jax.ShapeDtypeStruct((CONV_W - 1, rows, CONV_CH), F32)],
        compiler_params=_params("arbitrary"),
        name="conv_sample",
    )(proj, proj, proj, buf_t, conv_w)


def _bias_of_distance(n, bias_at):
    t = bias_at(_BUCKET0)
    for start, bucket in _BUCKET_STEPS:
        t = jnp.where(n >= start, bias_at(bucket), t)
    return t


def _lambda_full(lq1_ref, lk1_ref, lq2_ref, lk2_ref, lam_init):
    s1 = jnp.sum(lq1_ref[...] * lk1_ref[...], axis=-1, keepdims=True)
    s2 = jnp.sum(lq2_ref[...] * lk2_ref[...], axis=-1, keepdims=True)
    return jnp.exp(s1) - jnp.exp(s2) + lam_init


ATT_T = 256


def _attn_prompt_body(bias_ref, q_ref, k_ref, v_ref, lq1_ref, lk1_ref, lq2_ref, lk2_ref, sub_ref, o_ref,
                      tb_ref, *, lam_init, nq):
    h = pl.program_id(0)
    b = pl.program_id(1)
    qi = pl.program_id(2)
    T = ATT_T

    @pl.when((b == 0) & (qi == 0))
    def _():
        n0 = lax.broadcasted_iota(jnp.int32, (T, T), 0) - lax.broadcasted_iota(jnp.int32, (T, T), 1)
        bias_at = lambda bucket: bias_ref[bucket, h]
        tb_ref[0] = jnp.where(n0 >= 0, _bias_of_distance(n0, bias_at), NEG_INF)
        tb_ref[1] = _bias_of_distance(n0 + T, bias_at)

    q = q_ref[...]
    far_bias = bias_ref[NUM_BUCKETS - 1, h]
    lam = _lambda_full(lq1_ref, lk1_ref, lq2_ref, lk2_ref, lam_init)

    def one_map(c, n):
        cols = slice(c * ATTN_HD, (c + 1) * ATTN_HD)
        s = _dot_nt(q[:, cols], k_ref[0:(n + 1) * T, cols])
        far = [s[:, t * T:(t + 1) * T] for t in range(n - 1)]
        near = [s[:, t * T:(t + 1) * T] + tb_ref[n - t] for t in range(max(n - 1, 0), n + 1)]
        mx = near[0]
        for x in near[1:]:
            mx = jnp.maximum(mx, x)
        m = jnp.max(mx, axis=-1, keepdims=True)
        if far:
            fx = far[0]
            for x in far[1:]:
                fx = jnp.maximum(fx, x)
            m = jnp.maximum(m, jnp.max(fx, axis=-1, keepdims=True) + far_bias)
        m_far = m - far_bias
        ps = [jnp.exp(x - m_far) for x in far] + [jnp.exp(x - m) for x in near]
        tot = ps[0]
        for x in ps[1:]:
            tot = tot + x
        l = jnp.sum(tot, axis=-1, keepdims=True)
        p_all = jnp.concatenate([x.astype(BF16) for x in ps], axis=1)
        return _dot(p_all, v_ref[0:(n + 1) * T, :]) * (1.0 / l)

    for n in range(nq):
        @pl.when(qi == n)
        def _():
            o = one_map(0, n) - lam * one_map(1, n)
            o_ref[...] = (_rms_rows(o, sub_ref[...]) * (1.0 - lam_init)).astype(o_ref.dtype)


def attn_prompt(qb, kb, vb, rel_bias, lams, subln, a, lam_init, batch, seq):
    m = qb.shape[0]
    T = ATT_T
    nq = seq // T
    lam_spec = pl.BlockSpec((None, 1, ATTN_HD), lambda h, b, i: (a, 0, 0))
    return pl.pallas_call(
        functools.partial(_attn_prompt_body, lam_init=lam_init, nq=nq),
        grid=(ATTN_HEADS, batch, nq),
        in_specs=[pl.BlockSpec(memory_space=pltpu.SMEM),
                  pl.BlockSpec((T, ATTN_VD), lambda h, b, i: (b * nq + i, h)),
                  pl.BlockSpec((seq, ATTN_VD), lambda h, b, i: (b, h)),
                  pl.BlockSpec((seq, ATTN_VD), lambda h, b, i: (b, h)),
                  lam_spec, lam_spec, lam_spec, lam_spec,
                  pl.BlockSpec((None, 1, ATTN_VD), lambda h, b, i: (a, 0, 0))],
        out_specs=pl.BlockSpec((T, ATTN_VD), lambda h, b, i: (b * nq + i, h)),
        out_shape=jax.ShapeDtypeStruct((m, ATTN_HEADS * ATTN_VD), BF16),
        scratch_shapes=[pltpu.VMEM((2, T, T), F32)],
        compiler_params=_params("arbitrary", "arbitrary", "arbitrary"),
        name="attn_prompt",
    )(rel_bias, qb, kb, vb, *lams, subln)


SAMPLE_PAGES_PER_STEP = 4


def _attn_sample_body(pt_ref, qm_ref, *refs, lam_init, n_groups):
    G = SAMPLE_PAGES_PER_STEP
    k_refs, v_refs = refs[:G], refs[G:2 * G]
    (kn_ref, vn_ref, bias_ref, lq1_ref, lk1_ref, lq2_ref, lk2_ref, sub_ref, o_ref, m_ref, l_ref, acc_ref) = refs[2 * G:]
    g = pl.program_id(1)
    H = ATTN_HEADS
    R = 2 * H
    P = PAGE_SIZE
    qm = qm_ref[...]

    @pl.when(g == 0)
    def _():
        m_ref[...] = jnp.full(m_ref.shape, -jnp.inf, F32)
        l_ref[...] = jnp.zeros_like(l_ref)
        acc_ref[...] = jnp.zeros_like(acc_ref)

    def update(k_list, v_list, bias_list):
        n = k_list[0].shape[0]
        own = (lax.broadcasted_iota(jnp.int32, (R, n), 0) % H) == (lax.broadcasted_iota(jnp.int32, (R, n), 1) % H)
        s_list = [jnp.where(own, _dot_nt(qm, k2d) + bias, NEG_INF) for k2d, bias in zip(k_list, bias_list)]
        m_old = m_ref[...]
        m_new = m_old
        for s in s_list:
            m_new = jnp.maximum(m_new, jnp.max(s, axis=-1, keepdims=True))
        alpha = jnp.exp(m_old - m_new)
        l_new = alpha * l_ref[...]
        acc_new = alpha * acc_ref[...]
        for s, v2d in zip(s_list, v_list):
            p = jnp.exp(s - m_new)
            l_new = l_new + jnp.sum(p, axis=-1, keepdims=True)
            acc_new = acc_new + _dot(p.astype(BF16), v2d)
        m_ref[...] = m_new
        l_ref[...] = l_new
        acc_ref[...] = acc_new

    def pages(bias_list):
        k_list = [r[...].reshape(P * H, 2 * ATTN_HD).astype(BF16) for r in k_refs]
        v_list = [r[...].reshape(P * H, ATTN_VD).astype(BF16) for r in v_refs]
        update(k_list, v_list, bias_list)

    far_bias = bias_ref[:, NUM_BUCKETS - 1:NUM_BUCKETS]

    @pl.when(g < n_groups - 1)
    def _():
        pages([far_bias] * G)

    @pl.when(g == n_groups - 1)
    def _():
        dist = P - lax.broadcasted_iota(jnp.int32, (R, P * H), 1) // H
        near = _bias_of_distance(dist, lambda bucket: bias_ref[:, bucket:bucket + 1])
        pages([far_bias] * (G - 1) + [near])

    @pl.when(g == n_groups)
    def _():
        update([kn_ref[...].astype(BF16)], [vn_ref[...].astype(BF16)], [bias_ref[:, 0:1]])
        lam = _lambda_full(lq1_ref, lk1_ref, lq2_ref, lk2_ref, lam_init)
        on = acc_ref[...] * (1.0 / l_ref[...])
        o = on[:H] - lam * on[H:]
        o_ref[...] = _rms_rows(o, sub_ref[...]) * (1.0 - lam_init)


def attn_sample(qm, cache_k, cache_v, k_new, v_new, page_table, bias_rows, lams, subln, a, lam_init):
    batch, n_pages = page_table.shape
    H, P, G = ATTN_HEADS, PAGE_SIZE, SAMPLE_PAGES_PER_STEP
    R = 2 * H
    n_groups = n_pages // G

    def page_spec(t, width):
        return pl.BlockSpec((None, None, P, H, width),
                            lambda b, g, pt: (a, pt[b, jnp.minimum(g * G + t, n_pages - 1)], 0, 0, 0))

    lam_spec = pl.BlockSpec((None, 1, ATTN_HD), lambda b, g, pt: (a, 0, 0))
    grid_spec = pltpu.PrefetchScalarGridSpec(
        num_scalar_prefetch=1,
        grid=(batch, n_groups + 1),
        in_specs=([pl.BlockSpec((None, R, 2 * ATTN_HD), lambda b, g, pt: (b, 0, 0))]
                  + [page_spec(t, 2 * ATTN_HD) for t in range(G)]
                  + [page_spec(t, ATTN_VD) for t in range(G)]
                  + [pl.BlockSpec((None, H, 2 * ATTN_HD), lambda b, g, pt: (b, 0, 0)),
                     pl.BlockSpec((None, H, ATTN_VD), lambda b, g, pt: (b, 0, 0)),
                     pl.BlockSpec((R, NUM_BUCKETS), lambda b, g, pt: (0, 0)),
                     lam_spec, lam_spec, lam_spec, lam_spec,
                     pl.BlockSpec((None, 1, ATTN_VD), lambda b, g, pt: (a, 0, 0))]),
        out_specs=pl.BlockSpec((None, H, ATTN_VD), lambda b, g, pt: (b, 0, 0)),
        scratch_shapes=[pltpu.VMEM((R, 1), F32), pltpu.VMEM((R, 1), F32), pltpu.VMEM((R, ATTN_VD), F32)],
    )
    return pl.pallas_call(
        functools.partial(_attn_sample_body, lam_init=lam_init, n_groups=n_groups),
        grid_spec=grid_spec,
        out_shape=jax.ShapeDtypeStruct((batch, H, ATTN_VD), F32),
        compiler_params=_params("arbitrary", "arbitrary"),
        name="attn_sample",
    )(page_table, qm, *([cache_k] * G), *([cache_v] * G), k_new, v_new, bias_rows, *lams, subln)


def _pad_rows(x, rows):
    return jnp.pad(x, ((0, rows - x.shape[0]), (0, 0)))


def kernel(x_prompt, x_sample, cache_k, cache_v, state_gla, state_conv, page_table, rel_bias, norm_ffn1, w_ffn1_gu, w_ffn1_down, norm_mix, norm_ffn2, w_ffn2_gu, w_ffn2_down, norm_final, even_w_in, gla_w_alpha, gla_b_alpha, gla_out_norm, short_conv_w, even_w_out, attn_w_qkv, attn_lambda_q1, attn_lambda_k1, attn_lambda_q2, attn_lambda_k2, attn_subln, attn_w_o):
    bp, tp, d = x_prompt.shape
    bs = x_sample.shape[0]
    depth = norm_ffn1.shape[0]
    n_even = even_w_in.shape[0]
    n_odd = attn_w_qkv.shape[0]

    w1_gu, w1_d, w2_gu, w2_d = w_ffn1_gu, w_ffn1_down, w_ffn2_gu, w_ffn2_down
    a0 = 2 * GLA_QK + 2 * GLA_V
    w_in_main = jnp.concatenate([even_w_in[:, :, :a0], even_w_in[:, :, a0 + GLA_RANK:]], axis=-1).astype(BF16)
    w_in_alo = jnp.pad(even_w_in[:, :, a0:a0 + GLA_RANK], ((0, 0), (0, 0), (0, LANES - GLA_RANK))).astype(BF16)
    w_alpha = jnp.pad(gla_w_alpha, ((0, 0), (0, LANES - GLA_RANK), (0, 0))).astype(BF16)
    b_alpha = gla_b_alpha.reshape(n_even, 1, GLA_QK)
    out_gain = gla_out_norm.reshape(n_even, 1, GLA_DV)
    w_even_out, w_qkv, w_o = even_w_out, attn_w_qkv, attn_w_o
    lams = tuple(v.reshape(n_odd, 1, ATTN_HD) for v in (attn_lambda_q1, attn_lambda_k1, attn_lambda_q2, attn_lambda_k2))
    subln = attn_subln.reshape(n_odd, 1, ATTN_VD)
    q_scale = jnp.full((1, ATTN_QK), ATTN_SCALE, F32)
    bias_rows = jnp.tile(rel_bias.T, (2, 1))
    n_in = w_in_main.shape[-1]

    tiles = (bp * tp) // PROMPT_TM
    xs_pad = _pad_rows(x_sample.reshape(bs, d), SAMPLE_ROWS)
    x = jnp.concatenate([x_prompt.reshape(tiles, PROMPT_TM, d),
                         jnp.broadcast_to(xs_pad[None], (tiles, SAMPLE_ROWS, d))], axis=1).reshape(tiles * TILE_ROWS, d)
    conv_t = jnp.pad(jnp.swapaxes(state_conv, 1, 2), ((0, 0), (0, 0), (0, SAMPLE_ROWS - bs), (0, 0)))

    k_p, v_p, gla_p, conv_p = [], [], [], []
    k_s, v_s, gla_s, conv_s = [], [], [], []

    for l in range(depth):
        x, hn = half_ffn(x, norm_ffn1[l], w1_gu, w1_d, l, TILE_ROWS, next_gain=norm_mix[l])
        if l % 2 == 0:
            e = l // 2
            ((proj_p, proj_s),) = proj_matmul(hn, w_in_main, e, 0, n_in, (F32,))
            ((alo_p, alo_s),) = proj_matmul(hn, w_in_alo, e, 0, LANES, (F32,))
            og_p, st_p = gla_prompt(proj_p, alo_p, w_alpha, b_alpha, out_gain, e, bp, tp)
            oc_p, buf_p = conv_prompt(proj_p, short_conv_w, e, bp, tp)
            og_s, st_s = gla_sample(proj_s, alo_s, w_alpha, b_alpha, out_gain, state_gla, e, bs)
            oc_s, buf_s = conv_sample(proj_s, conv_t[e], short_conv_w, e)
            og_s = _pad_rows(og_s.reshape(bs, GLA_V), SAMPLE_ROWS).astype(BF16)
            x = out_proj_matmul([og_p, oc_p], [og_s, oc_s.astype(BF16)], w_even_out, e, x)
            gla_p.append(st_p)
            conv_p.append(buf_p)
            gla_s.append(st_s)
            conv_s.append(jnp.swapaxes(buf_s[:, :bs], 0, 1))
        else:
            a = l // 2
            lam_init = 0.8 - 0.6 * math.exp(-0.3 * l)
            ((qb_p, qb_s),) = proj_matmul(hn, w_qkv, a, 0, ATTN_QK, (BF16,), scale=q_scale)
            (kf_p, kf_s), (kb_p, _) = proj_matmul(hn, w_qkv, a, ATTN_QK, ATTN_QK, (F32, BF16))
            (vf_p, vf_s), (vb_p, _) = proj_matmul(hn, w_qkv, a, 2 * ATTN_QK, ATTN_QK, (F32, BF16))
            o_p = attn_prompt(qb_p, kb_p, vb_p, rel_bias, lams, subln, a, lam_init, bp, tp)
            q4 = qb_s[:bs].reshape(bs, ATTN_HEADS, 2, ATTN_HD)
            z4 = jnp.zeros_like(q4[:, :, 0])
            qm = jnp.concatenate([jnp.concatenate([q4[:, :, 0], z4], axis=-1),
                                  jnp.concatenate([z4, q4[:, :, 1]], axis=-1)], axis=1)
            kn = kf_s[:bs].reshape(bs, ATTN_HEADS, 2 * ATTN_HD)
            vn = vf_s[:bs].reshape(bs, ATTN_HEADS, ATTN_VD)
            o_s = attn_sample(qm, cache_k, cache_v, kn, vn, page_table, bias_rows, lams, subln, a, lam_init)
            o_s = _pad_rows(o_s.reshape(bs, ATTN_HEADS * ATTN_VD), SAMPLE_ROWS).astype(BF16)
            x = out_proj_matmul([o_p], [o_s], w_o, a, x)
            k_p.append(kf_p.reshape(bp, tp, ATTN_HEADS, 2 * ATTN_HD))
            v_p.append(vf_p.reshape(bp, tp, ATTN_HEADS, ATTN_VD))
            k_s.append(kn.reshape(bs, 1, ATTN_HEADS, 2 * ATTN_HD))
            v_s.append(vn.reshape(bs, 1, ATTN_HEADS, ATTN_VD))
        x = half_ffn(x, norm_ffn2[l], w2_gu, w2_d, l, TILE_ROWS)

    x3 = x.reshape(tiles, TILE_ROWS, d)
    y_p = rmsnorm_untile(x3, norm_final, 0, PROMPT_TM, 256).reshape(bp, tp, d)
    y_s = rmsnorm_untile(x3[:1], norm_final, PROMPT_TM, SAMPLE_ROWS, SAMPLE_ROWS)[:bs].reshape(bs, 1, d)
    return (y_p, y_s,
            jnp.stack(k_p), jnp.stack(v_p), jnp.stack(gla_p), jnp.stack(conv_p),
            jnp.stack(k_s), jnp.stack(v_s), jnp.stack(gla_s), jnp.stack(conv_s))
```
